```python
import math
import jax, jax.numpy as jnp
from jax import lax
import numpy as np

D_MODEL = 1024
BATCH = 2
SEQ = 8192
DEPTH = 4
DEC_BATCH = 32
DEC_SEQ = 4
PAST_LEN = 8192
PAGE_SIZE = 128

N_HEADS = 8
HEAD_DIM = 64
D_ATTN = N_HEADS * HEAD_DIM
N_REC_HEADS = 8
REC_BLOCK = 64
D_REC = N_REC_HEADS * REC_BLOCK
D_IN = 3 * D_ATTN + 2 * D_REC
CONV_W = 4
LRU_C = 8.0
MOBA_BLOCK = 256
MOBA_TOPK = 3
Q_CHUNK = 64
D_FF = ((8 * D_MODEL // 3 + 255) // 256) * 256
ROPE_THETA = 10000.0
EPS = 1e-6
NEG = -1e30

kernel_name = 'hymba_moba_rglru_step'


def _rmsnorm(x, g):
    xf = x.astype(jnp.float32)
    y = xf * lax.rsqrt(jnp.mean(xf * xf, axis=-1, keepdims=True) + EPS)
    return (y * g.astype(jnp.float32)).astype(x.dtype)


def _rotary(x, pos):
    half = HEAD_DIM // 2
    inv = ROPE_THETA ** (-jnp.arange(half, dtype=jnp.float32) / half)
    ang = pos.astype(jnp.float32)[:, None] * inv[None, :]
    cos = jnp.cos(ang)[None, :, None, :]
    sin = jnp.sin(ang)[None, :, None, :]
    xf = x.astype(jnp.float32)
    x1, x2 = xf[..., :half], xf[..., half:]
    return jnp.concatenate([x1 * cos - x2 * sin, x2 * cos + x1 * sin], axis=-1).astype(x.dtype)


def _causal_conv(xr, prev, w, b):
    s = xr.shape[1]
    xp = jnp.concatenate([prev, xr], axis=1)
    y = xp[:, 0:s] * w[0]
    for t in range(1, CONV_W):
        y = y + xp[:, t:t + s] * w[t]
    y = y + b
    return y, xp[:, -(CONV_W - 1):]


def _rg_lru(xc, h0, wa, ba, wx, bx, lam):
    bsz, s, _ = xc.shape
    xf = xc.astype(jnp.float32)
    xh = xf.reshape(bsz, s, N_REC_HEADS, REC_BLOCK)
    r = jax.nn.sigmoid(jnp.einsum('bshi,hij->bshj', xh, wa.astype(jnp.float32)).reshape(bsz, s, D_REC) + ba.astype(jnp.float32))
    gi = jax.nn.sigmoid(jnp.einsum('bshi,hij->bshj', xh, wx.astype(jnp.float32)).reshape(bsz, s, D_REC) + bx.astype(jnp.float32))
    log_a = -LRU_C * r * jax.nn.softplus(-lam.astype(jnp.float32))
    a = jnp.exp(log_a)
    bterm = jnp.sqrt(-jnp.expm1(2.0 * log_a)) * (gi * xf)

    def comb(left, right):
        a_l, b_l = left
        a_r, b_r = right
        return a_l * a_r, a_r * b_l + b_r

    a_cum, b_cum = lax.associative_scan(comb, (a, bterm), axis=1)
    h = a_cum * h0.astype(jnp.float32)[:, None, :] + b_cum
    return h.astype(xc.dtype), h[:, -1].astype(xc.dtype)


def _blocks(k, v):
    t = k.shape[1]
    nb = -(-t // MOBA_BLOCK)
    pad = nb * MOBA_BLOCK - t
    kp = jnp.pad(k, ((0, 0), (0, pad), (0, 0), (0, 0)))
    vp = jnp.pad(v, ((0, 0), (0, pad), (0, 0), (0, 0)))
    bsz = k.shape[0]
    kb = kp.reshape(bsz, nb, MOBA_BLOCK, N_HEADS, HEAD_DIM).transpose(0, 3, 1, 2, 4)
    vb = vp.reshape(bsz, nb, MOBA_BLOCK, N_HEADS, HEAD_DIM).transpose(0, 3, 1, 2, 4)
    km = jnp.mean(kb, axis=3, dtype=jnp.float32)
    return kb, vb, km


def _moba_chunk(qc, pos_c, kb, vb, km):
    bsz, nh, nq, _ = qc.shape
    n_blocks = kb.shape[2]
    kk = min(MOBA_TOPK, n_blocks)
    own = pos_c // MOBA_BLOCK
    s = jnp.einsum('bhqd,bhnd->bhqn', qc.astype(jnp.float32), km)
    past = jnp.arange(n_blocks)[None, :] < own[:, None]
    s = jnp.where(past[None, None], s, NEG)
    _, top = lax.top_k(s, kk)
    own_b = jnp.broadcast_to(own[None, None, :, None], (bsz, nh, nq, 1)).astype(top.dtype)
    idx = jnp.concatenate([top, own_b], axis=-1)
    slot_ok = jnp.concatenate([jnp.arange(kk)[None, :] < own[:, None], jnp.ones((nq, 1), dtype=bool)], axis=-1)
    kpos = idx[..., None] * MOBA_BLOCK + jnp.arange(MOBA_BLOCK)
    mask = slot_ok[None, None, :, :, None] & (kpos <= pos_c[None, None, :, None, None])
    gather = jax.vmap(jax.vmap(lambda tab, ix: tab[ix]))
    kg = gather(kb, idx)
    vg = gather(vb, idx)
    logits = jnp.einsum('bhqd,bhqnkd->bhqnk', qc, kg, preferred_element_type=jnp.float32)
    logits = jnp.where(mask, logits, NEG).reshape(bsz, nh, nq, -1)
    p = jax.nn.softmax(logits, axis=-1).reshape(kg.shape[:-1]).astype(vg.dtype)
    return jnp.einsum('bhqnk,bhqnkd->bhqd', p, vg)


def _mixer(h, pos, k_past, v_past, conv_prev, h_prev, w_in, conv_w, conv_b, wa, ba, wx, bx, lam, w_out, q_chunk):
    bsz, s, _ = h.shape
    proj = h @ w_in
    q = proj[..., :D_ATTN]
    k = proj[..., D_ATTN:2 * D_ATTN]
    v = proj[..., 2 * D_ATTN:3 * D_ATTN]
    xr = proj[..., 3 * D_ATTN:3 * D_ATTN + D_REC]
    gr = proj[..., 3 * D_ATTN + D_REC:]
    q = _rotary(q.reshape(bsz, s, N_HEADS, HEAD_DIM), pos) * (HEAD_DIM ** -0.5)
    k = _rotary(k.reshape(bsz, s, N_HEADS, HEAD_DIM), pos)
    v = v.reshape(bsz, s, N_HEADS, HEAD_DIM)
    if k_past is None:
        k_all, v_all = k, v
    else:
        k_all = jnp.concatenate([k_past, k], axis=1)
        v_all = jnp.concatenate([v_past, v], axis=1)
    kb, vb, km = _blocks(k_all, v_all)
    qh = q.transpose(0, 2, 1, 3)
    if q_chunk >= s:
        o = _moba_chunk(qh, pos, kb, vb, km)
    else:
        nc = s // q_chunk
        qs = qh.reshape(bsz, N_HEADS, nc, q_chunk, HEAD_DIM).transpose(2, 0, 1, 3, 4)
        ps = pos.reshape(nc, q_chunk)
        o = lax.map(lambda a: _moba_chunk(a[0], a[1], kb, vb, km), (qs, ps))
        o = o.transpose(1, 2, 0, 3, 4).reshape(bsz, N_HEADS, s, HEAD_DIM)
    attn = o.transpose(0, 2, 1, 3).reshape(bsz, s, D_ATTN)
    xc, conv_new = _causal_conv(xr, conv_prev, conv_w, conv_b)
    hr, h_last = _rg_lru(xc, h_prev, wa, ba, wx, bx, lam)
    rec = hr * jax.nn.gelu(gr)
    out = jnp.concatenate([attn, rec], axis=-1) @ w_out
    return out, k, v, h_last, conv_new


def _ffn(h, wg, wu, wd):
    return (jax.nn.silu(h @ wg) * (h @ wu)) @ wd


def setup_inputs(seed: int = 0) -> dict:
    key = jax.random.key(seed)
    ks = jax.random.split(key, 24)
    f32 = jnp.float32
    n_pages = PAST_LEN // PAGE_SIZE
    n_phys = (DEC_BATCH * n_pages * 5) // 4
    x_prompt = jax.random.normal(ks[0], (BATCH, SEQ, D_MODEL), f32)
    x_sample = jax.random.normal(ks[1], (DEC_BATCH, DEC_SEQ, D_MODEL), f32)
    cache_k = jax.random.normal(ks[2], (DEPTH, n_phys, PAGE_SIZE, N_HEADS, HEAD_DIM), f32)
    cache_v = jax.random.normal(ks[3], (DEPTH, n_phys, PAGE_SIZE, N_HEADS, HEAD_DIM), f32)
    state_h = jax.random.normal(ks[4], (DEPTH, DEC_BATCH, D_REC), f32) * 0.5
    state_conv = jax.random.normal(ks[5], (DEPTH, DEC_BATCH, CONV_W - 1, D_REC), f32)
    page_table = jax.random.permutation(ks[6], n_phys)[:DEC_BATCH * n_pages].reshape(DEC_BATCH, n_pages).astype(jnp.int32)
    norm_mix = 1.0 + 0.05 * jax.random.normal(ks[7], (DEPTH, D_MODEL), f32)
    w_in = jax.random.normal(ks[8], (DEPTH, D_MODEL, D_IN), f32) * D_MODEL ** -0.5
    conv_w = jax.random.normal(ks[9], (DEPTH, CONV_W, D_REC), f32) * CONV_W ** -0.5
    conv_b = 0.02 * jax.random.normal(ks[10], (DEPTH, D_REC), f32)
    w_gate_a = jax.random.normal(ks[11], (DEPTH, N_REC_HEADS, REC_BLOCK, REC_BLOCK), f32) * REC_BLOCK ** -0.5
    b_gate_a = 0.02 * jax.random.normal(ks[12], (DEPTH, D_REC), f32)
    w_gate_x = jax.random.normal(ks[13], (DEPTH, N_REC_HEADS, REC_BLOCK, REC_BLOCK), f32) * REC_BLOCK ** -0.5
    b_gate_x = 0.02 * jax.random.normal(ks[14], (DEPTH, D_REC), f32)
    u = jax.random.uniform(ks[15], (DEPTH, D_REC), f32, minval=0.9, maxval=0.999)
    a0 = u ** (1.0 / LRU_C)
    lru_lambda = jnp.log(a0) - jnp.log1p(-a0)
    w_out = jax.random.normal(ks[16], (DEPTH, D_MODEL, D_MODEL), f32) * D_MODEL ** -0.5
    norm_ffn = 1.0 + 0.05 * jax.random.normal(ks[17], (DEPTH, D_MODEL), f32)
    w_ff_gate = jax.random.normal(ks[18], (DEPTH, D_MODEL, D_FF), f32) * D_MODEL ** -0.5
    w_ff_up = jax.random.normal(ks[19], (DEPTH, D_MODEL, D_FF), f32) * D_MODEL ** -0.5
    w_ff_down = jax.random.normal(ks[20], (DEPTH, D_FF, D_MODEL), f32) * D_FF ** -0.5
    norm_final = 1.0 + 0.05 * jax.random.normal(ks[21], (D_MODEL,), f32)
    return {'x_prompt': x_prompt, 'x_sample': x_sample, 'cache_k': cache_k, 'cache_v': cache_v,
            'state_h': state_h, 'state_conv': state_conv, 'page_table': page_table,
            'norm_mix': norm_mix, 'w_in': w_in, 'conv_w': conv_w, 'conv_b': conv_b,
            'w_gate_a': w_gate_a, 'b_gate_a': b_gate_a, 'w_gate_x': w_gate_x, 'b_gate_x': b_gate_x,
            'lru_lambda': lru_lambda, 'w_out': w_out, 'norm_ffn': norm_ffn,
            'w_ff_gate': w_ff_gate, 'w_ff_up': w_ff_up, 'w_ff_down': w_ff_down, 'norm_final': norm_final}


def reference(x_prompt, x_sample, cache_k, cache_v, state_h, state_conv, page_table,
              norm_mix, w_in, conv_w, conv_b, w_gate_a, b_gate_a, w_gate_x, b_gate_x,
              lru_lambda, w_out, norm_ffn, w_ff_gate, w_ff_up, w_ff_down, norm_final):
    bp, sp, _ = x_prompt.shape
    bs, ss, _ = x_sample.shape
    past = page_table.shape[1] * cache_k.shape[2]
    pos_p = jnp.arange(sp, dtype=jnp.int32)
    pos_s = past + jnp.arange(ss, dtype=jnp.int32)
    zero_conv = jnp.zeros((bp, CONV_W - 1, D_REC), x_prompt.dtype)
    zero_h = jnp.zeros((bp, D_REC), x_prompt.dtype)
    xp, xs = x_prompt, x_sample
    kp_l, vp_l, hp_l, cp_l, ks_l, vs_l, hs_l, cs_l = [], [], [], [], [], [], [], []
    for l in range(DEPTH):
        lw = (w_in[l], conv_w[l], conv_b[l], w_gate_a[l], b_gate_a[l], w_gate_x[l], b_gate_x[l], lru_lambda[l], w_out[l])
        out, kn, vn, hn, cn = _mixer(_rmsnorm(xp, norm_mix[l]), pos_p, None, None, zero_conv, zero_h, *lw, Q_CHUNK)
        xp = xp + out
        xp = xp + _ffn(_rmsnorm(xp, norm_ffn[l]), w_ff_gate[l], w_ff_up[l], w_ff_down[l])
        kp_l.append(kn); vp_l.append(vn); hp_l.append(hn); cp_l.append(cn)
        k_past = cache_k[l, page_table].reshape(bs, past, N_HEADS, HEAD_DIM)
        v_past = cache_v[l, page_table].reshape(bs, past, N_HEADS, HEAD_DIM)
        out, kn, vn, hn, cn = _mixer(_rmsnorm(xs, norm_mix[l]), pos_s, k_past, v_past, state_conv[l], state_h[l], *lw, ss)
        xs = xs + out
        xs = xs + _ffn(_rmsnorm(xs, norm_ffn[l]), w_ff_gate[l], w_ff_up[l], w_ff_down[l])
        ks_l.append(kn); vs_l.append(vn); hs_l.append(hn); cs_l.append(cn)
    y_prompt = _rmsnorm(xp, norm_final)
    y_sample = _rmsnorm(xs, norm_final)
    new_k_prompt = jnp.stack(kp_l)
    new_v_prompt = jnp.stack(vp_l)
    new_h_prompt = jnp.stack(hp_l)
    new_conv_prompt = jnp.stack(cp_l)
    new_k_sample = jnp.stack(ks_l)
    new_v_sample = jnp.stack(vs_l)
    new_h_sample = jnp.stack(hs_l)
    new_conv_sample = jnp.stack(cs_l)
    return (y_prompt, y_sample, new_k_prompt, new_v_prompt, new_h_prompt, new_conv_prompt,
            new_k_sample, new_v_sample, new_h_sample, new_conv_sample)
```

```python
import functools

import jax
import jax.numpy as jnp
from jax import lax
from jax.experimental import pallas as pl
from jax.experimental.pallas import tpu as pltpu

D_MODEL = 1024
N_HEADS = 8
HEAD_DIM = 64
D_ATTN = N_HEADS * HEAD_DIM
N_REC_HEADS = 8
REC_BLOCK = 64
D_REC = N_REC_HEADS * REC_BLOCK
D_IN = 3 * D_ATTN + 2 * D_REC
CONV_W = 4
LRU_C = 8.0
MOBA_BLOCK = 256
MOBA_TOPK = 3
D_FF = 2816
ROPE_THETA = 10000.0
EPS = 1e-6
NEG = -1e30

LANES = 128
SUBLANES = 8
HEADS_PER_GROUP = LANES // HEAD_DIM
N_GROUPS = N_HEADS // HEADS_PER_GROUP
VMEM_LIMIT = 56 * 1024 * 1024

F32 = jnp.float32
BF16 = jnp.bfloat16
_NT = (((1,), (1,)), ((), ()))


def _params(semantics):
    return pltpu.CompilerParams(dimension_semantics=semantics, vmem_limit_bytes=VMEM_LIMIT)


def _resident(shape):
    zeros = (0,) * len(shape)
    return pl.BlockSpec(shape, lambda *_: zeros, pipeline_mode=pl.Buffered(1))


def _rmsnorm(x, g):
    y = x * lax.rsqrt(jnp.mean(x * x, axis=-1, keepdims=True) + EPS)
    return y * g


def _rope_angles(pos):
    half = HEAD_DIM // 2
    inv = ROPE_THETA ** (-jnp.arange(half, dtype=F32) / half)
    ang = pos.astype(F32)[:, None] * inv[None, :]
    return jnp.cos(ang), jnp.sin(ang)


def _rope_table(pos):
    cos, sin = _rope_angles(pos)
    zero = jnp.zeros_like(sin)
    reps = LANES // HEAD_DIM
    cos_t = jnp.tile(cos, (1, 2 * reps))
    sin_a = jnp.tile(jnp.concatenate([-sin, zero], axis=1), (1, reps))
    sin_b = jnp.tile(jnp.concatenate([zero, sin], axis=1), (1, reps))
    return jnp.concatenate([cos_t, sin_a, sin_b], axis=1)


def _rope_table_t(pos):
    cos, sin = _rope_angles(pos)
    return jnp.concatenate([cos, cos, -sin, sin], axis=1).T


def _rope_lanes(t, tab):
    cos, sin_a, sin_b = tab[:, :LANES], tab[:, LANES:2 * LANES], tab[:, 2 * LANES:]
    half = HEAD_DIM // 2
    pieces = []
    for j in range(D_ATTN // LANES):
        tj = t[:, j * LANES:(j + 1) * LANES]
        pieces.append(tj * cos + pltpu.roll(tj, LANES - half, axis=1) * sin_a
                      + pltpu.roll(tj, half, axis=1) * sin_b)
    return jnp.concatenate(pieces, axis=1)


def _rope_rows(t, tab_t):
    cos, sin = tab_t[:HEAD_DIM], tab_t[HEAD_DIM:]
    half = HEAD_DIM // 2
    pieces = []
    for h in range(N_HEADS):
        th = t[h * HEAD_DIM:(h + 1) * HEAD_DIM]
        partner = jnp.concatenate([th[half:], th[:half]], axis=0)
        pieces.append(th * cos + partner * sin)
    return jnp.concatenate(pieces, axis=0)


def _proj_prompt_kernel(x_ref, g_ref, wq_ref, wkv_ref, wr_ref, tab_ref, tabt_ref,
                        q_ref, kt_ref, vt_ref, xr_ref, gr_ref):
    hn = _rmsnorm(x_ref[...], g_ref[...]).astype(BF16)
    q = jnp.dot(hn, wq_ref[...], preferred_element_type=F32)
    q_ref[...] = (_rope_lanes(q, tab_ref[...]) * (HEAD_DIM ** -0.5)).astype(BF16)
    kv_t = lax.dot_general(wkv_ref[...], hn, _NT, preferred_element_type=F32)
    kt_ref[0] = _rope_rows(kv_t[:D_ATTN], tabt_ref[...])
    vt_ref[0] = kv_t[D_ATTN:]
    r = jnp.dot(hn, wr_ref[...], preferred_element_type=F32)
    xr_ref[...] = r[:, :D_REC]
    gr_ref[...] = r[:, D_REC:]


def _proj_prompt(x, g, wq, wkv_t, wr, tab, tab_t, *, bsz, seq, tm):
    n = bsz * seq
    tps = seq // tm
    row = lambda b, j: (b * tps + j, 0)
    col = lambda b, j: (b, 0, j)
    return pl.pallas_call(
        _proj_prompt_kernel,
        grid=(bsz, tps),
        in_specs=[pl.BlockSpec((tm, D_MODEL), row),
                  _resident((1, D_MODEL)),
                  _resident((D_MODEL, D_ATTN)),
                  _resident((2 * D_ATTN, D_MODEL)),
                  _resident((D_MODEL, 2 * D_REC)),
                  pl.BlockSpec((tm, 3 * LANES), lambda b, j: (j, 0)),
                  pl.BlockSpec((2 * HEAD_DIM, tm), lambda b, j: (0, j))],
        out_specs=[pl.BlockSpec((tm, D_ATTN), row),
                   pl.BlockSpec((1, D_ATTN, tm), col), pl.BlockSpec((1, D_ATTN, tm), col),
                   pl.BlockSpec((tm, D_REC), row), pl.BlockSpec((tm, D_REC), row)],
        out_shape=[jax.ShapeDtypeStruct((n, D_ATTN), BF16),
                   jax.ShapeDtypeStruct((bsz, D_ATTN, seq), F32),
                   jax.ShapeDtypeStruct((bsz, D_ATTN, seq), F32),
                   jax.ShapeDtypeStruct((n, D_REC), F32), jax.ShapeDtypeStruct((n, D_REC), F32)],
        compiler_params=_params(("parallel", "parallel")),
        name="proj_prompt",
    )(x, g, wq, wkv_t, wr, tab, tab_t)


def _proj_sample_kernel(x_ref, g_ref, w_ref, tab_ref, q_ref, k_ref, v_ref, xr_ref, gr_ref):
    hn = _rmsnorm(x_ref[...], g_ref[...]).astype(BF16)
    proj = jnp.dot(hn, w_ref[...], preferred_element_type=F32)
    tab = tab_ref[...]
    q_ref[...] = (_rope_lanes(proj[:, :D_ATTN], tab) * (HEAD_DIM ** -0.5)).astype(BF16)
    k_ref[...] = _rope_lanes(proj[:, D_ATTN:2 * D_ATTN], tab)
    v_ref[...] = proj[:, 2 * D_ATTN:3 * D_ATTN]
    xr_ref[...] = proj[:, 3 * D_ATTN:3 * D_ATTN + D_REC]
    gr_ref[...] = proj[:, 3 * D_ATTN + D_REC:]


def _proj_sample(x, g, w_bf, tab):
    n = x.shape[0]
    full = lambda shape: pl.BlockSpec(shape, lambda i: (0,) * len(shape))
    return pl.pallas_call(
        _proj_sample_kernel,
        grid=(1,),
        in_specs=[full((n, D_MODEL)), full((1, D_MODEL)), full((D_MODEL, D_IN)), full((n, 3 * LANES))],
        out_specs=[full((n, D_ATTN))] * 3 + [full((n, D_REC))] * 2,
        out_shape=[jax.ShapeDtypeStruct((n, D_ATTN), BF16),
                   jax.ShapeDtypeStruct((n, D_ATTN), F32), jax.ShapeDtypeStruct((n, D_ATTN), F32),
                   jax.ShapeDtypeStruct((n, D_REC), F32), jax.ShapeDtypeStruct((n, D_REC), F32)],
        compiler_params=_params(("arbitrary",)),
        name="proj_sample",
    )(x, g, w_bf, tab)


def _top3_bias(scores, lane_f, n_valid):
    sg = jnp.where(lane_f < n_valid, scores, NEG)
    bias = jnp.full(sg.shape, NEG, F32)
    for _ in range(MOBA_TOPK):
        m = jnp.max(sg, axis=1, keepdims=True)
        idx = jnp.min(jnp.where(sg == m, lane_f, float(LANES)), axis=1, keepdims=True)
        pick = lane_f == idx
        bias = jnp.where(pick & (m > 0.5 * NEG), 0.0, bias)
        sg = jnp.where(pick, NEG, sg)
    return bias


def _moba_prompt_kernel(q_ref, kt_ref, vt_ref, o_ref, kb_ref, vb_ref, km_ref):
    own = pl.program_id(2)
    blk = MOBA_BLOCK
    nblk = kb_ref.shape[0]

    @pl.when(own == 0)
    def _():
        lane_k = lax.broadcasted_iota(jnp.int32, (LANES, LANES), 1)
        km = jnp.zeros((LANES, LANES), F32)
        for n in range(nblk):
            kn = kt_ref[0, :, n * blk:(n + 1) * blk]
            kb_ref[n] = kn.astype(BF16)
            vb_ref[n] = vt_ref[0, :, n * blk:(n + 1) * blk].astype(BF16)
            km = jnp.where(lane_k == n, jnp.mean(kn, axis=1, keepdims=True), km)
        km_ref[...] = km

    q2 = q_ref[...]
    lane = lax.broadcasted_iota(jnp.int32, (blk, LANES), 1)
    lane_f = lane.astype(F32)
    own_f = own.astype(F32)

    q_heads, q_aug = [], []
    for h in range(HEADS_PER_GROUP):
        in_head = (lane >= h * HEAD_DIM) & (lane < (h + 1) * HEAD_DIM)
        qh = jnp.where(in_head, q2, jnp.zeros_like(q2))
        gate = jnp.dot(qh.astype(F32), km_ref[...], precision=lax.Precision.HIGHEST,
                       preferred_element_type=F32)
        bias = _top3_bias(gate, lane_f, own_f)
        q_heads.append(qh)
        q_aug.append(jnp.concatenate([qh, bias.astype(BF16)], axis=1))

    kd = kb_ref[own]
    vd = vb_ref[own]
    r_i = lax.broadcasted_iota(jnp.int32, (blk, blk), 0)
    c_i = lax.broadcasted_iota(jnp.int32, (blk, blk), 1)
    causal = c_i <= r_i
    state = []
    for h in range(HEADS_PER_GROUP):
        s = jnp.dot(q_heads[h], kd, preferred_element_type=F32)
        s = jnp.where(causal, s, NEG)
        m = jnp.max(s, axis=1, keepdims=True)
        p = jnp.exp(s - m)
        l = jnp.sum(p, axis=1, keepdims=True)
        acc = lax.dot_general(p.astype(BF16), vd, _NT, preferred_element_type=F32)
        state += [m, l, acc]

    sub = lax.broadcasted_iota(jnp.int32, (LANES, blk), 0)

    def body(n, carry):
        onehot = jnp.where(sub == n, 1.0, 0.0).astype(BF16)
        k_aug = jnp.concatenate([kb_ref[n], onehot], axis=0)
        vb = vb_ref[n]
        new = []
        for h in range(HEADS_PER_GROUP):
            m, l, acc = carry[3 * h:3 * h + 3]
            s = jnp.dot(q_aug[h], k_aug, preferred_element_type=F32)
            m_new = jnp.maximum(m, jnp.max(s, axis=1, keepdims=True))
            alpha = jnp.exp(m - m_new)
            p = jnp.exp(s - m_new)
            l = alpha * l + jnp.sum(p, axis=1, keepdims=True)
            acc = alpha * acc + lax.dot_general(p.astype(BF16), vb, _NT, preferred_element_type=F32)
            new += [m_new, l, acc]
        return tuple(new)

    carry = lax.fori_loop(0, own, body, tuple(state))
    out = None
    for h in range(HEADS_PER_GROUP):
        _, l, acc = carry[3 * h:3 * h + 3]
        oh = acc / l
        in_head = (lane >= h * HEAD_DIM) & (lane < (h + 1) * HEAD_DIM)
        out = oh if out is None else jnp.where(in_head, oh, out)
    o_ref[...] = out.astype(BF16)


def _moba_prompt(q_bf, k_t, v_t, *, bsz, seq):
    nblk = seq // MOBA_BLOCK
    n = bsz * seq
    assert nblk <= LANES
    return pl.pallas_call(
        _moba_prompt_kernel,
        grid=(bsz, N_GROUPS, nblk),
        in_specs=[pl.BlockSpec((MOBA_BLOCK, LANES), lambda b, g, i: (b * nblk + i, g)),
                  pl.BlockSpec((1, LANES, seq), lambda b, g, i: (b, g, 0)),
                  pl.BlockSpec((1, LANES, seq), lambda b, g, i: (b, g, 0))],
        out_specs=pl.BlockSpec((MOBA_BLOCK, LANES), lambda b, g, i: (b * nblk + i, g)),
        out_shape=jax.ShapeDtypeStruct((n, D_ATTN), BF16),
        scratch_shapes=[pltpu.VMEM((nblk, LANES, MOBA_BLOCK), BF16),
                        pltpu.VMEM((nblk, LANES, MOBA_BLOCK), BF16),
                        pltpu.VMEM((LANES, LANES), F32)],
        compiler_params=_params(("arbitrary", "arbitrary", "arbitrary")),
        name="moba_prompt",
    )(q_bf, k_t, v_t)


def _moba_sample_kernel(pt_ref, q_ref, kn_ref, vn_ref, ck_hbm, cv_hbm, o_ref,
                        buf, sem, s_ref, p_ref, *, layer, n_pages, page, pages_per_chunk):
    b = pl.program_id(0)
    n_q = q_ref.shape[1]
    rows = n_q * N_HEADS
    chunk = pages_per_chunk * page
    n_chunks = n_pages // pages_per_chunk
    blocks_per_chunk = chunk // MOBA_BLOCK
    nblk = n_pages * page // MOBA_BLOCK

    def chunk_copies(src, c, slot):
        return [pltpu.make_async_copy(src.at[layer, pt_ref[b, c * pages_per_chunk + j]],
                                      buf.at[slot, :, pl.ds(j * page, page)], sem.at[slot])
                for j in range(pages_per_chunk)]

    loads = [(ck_hbm, c) for c in range(n_chunks)] + [(cv_hbm, c) for c in range(n_chunks)]

    def start(i):
        for cp in chunk_copies(loads[i][0], loads[i][1], i % 2):
            cp.start()

    def wait(i):
        for cp in chunk_copies(loads[i][0], loads[i][1], i % 2):
            cp.wait()

    start(0)

    q4 = q_ref[0]
    row_i = lax.broadcasted_iota(jnp.int32, (rows, D_ATTN), 0)
    col_i = lax.broadcasted_iota(jnp.int32, (rows, D_ATTN), 1)
    own_head = (col_i // HEAD_DIM) == (row_i % N_HEADS)
    q_rep = jnp.concatenate([jnp.broadcast_to(q4[t:t + 1], (N_HEADS, D_ATTN)) for t in range(n_q)], axis=0)
    q_bd = jnp.where(own_head, q_rep, jnp.zeros_like(q_rep))

    lane = lax.broadcasted_iota(jnp.int32, (rows, LANES), 1)
    gate = jnp.zeros((rows, LANES), F32)
    for i in range(n_chunks):
        start(i + 1)
        wait(i)
        kc = buf[i % 2].astype(BF16)
        s_c = jnp.dot(q_bd, kc, preferred_element_type=F32)
        s_ref[:, i * chunk:(i + 1) * chunk] = s_c
        for j in range(blocks_per_chunk):
            tot = jnp.sum(s_c[:, j * MOBA_BLOCK:(j + 1) * MOBA_BLOCK], axis=1, keepdims=True)
            gate = jnp.where(lane == i * blocks_per_chunk + j, tot * (1.0 / MOBA_BLOCK), gate)

    bias = _top3_bias(gate, lane.astype(F32), float(nblk))

    tok = lax.broadcasted_iota(jnp.int32, (rows, 1), 0) // N_HEADS
    q_bd_f = q_bd.astype(F32)
    kn = kn_ref[0]
    vn = vn_ref[0]
    s_new = []
    for j in range(n_q):
        sj = jnp.sum(q_bd_f * kn[j:j + 1, :], axis=1, keepdims=True)
        s_new.append(jnp.where(tok >= j, sj, NEG))

    m = s_new[0]
    for sj in s_new[1:]:
        m = jnp.maximum(m, sj)
    bias_cols = []
    for n in range(nblk):
        bcol = jnp.max(jnp.where(lane == n, bias, NEG), axis=1, keepdims=True)
        bias_cols.append(bcol)
        sb = s_ref[:, n * MOBA_BLOCK:(n + 1) * MOBA_BLOCK] + bcol
        m = jnp.maximum(m, jnp.max(sb, axis=1, keepdims=True))
    p_new = [jnp.exp(sj - m) for sj in s_new]
    l = p_new[0]
    for pj in p_new[1:]:
        l = l + pj
    for n in range(nblk):
        pb = jnp.exp(s_ref[:, n * MOBA_BLOCK:(n + 1) * MOBA_BLOCK] + bias_cols[n] - m)
        l = l + jnp.sum(pb, axis=1, keepdims=True)
        p_ref[:, n * MOBA_BLOCK:(n + 1) * MOBA_BLOCK] = pb.astype(BF16)

    acc = jnp.zeros((rows, D_ATTN), F32)
    for i in range(n_chunks, 2 * n_chunks):
        if i + 1 < 2 * n_chunks:
            start(i + 1)
        wait(i)
        c = i - n_chunks
        vc = buf[i % 2].astype(BF16)
        acc = acc + lax.dot_general(p_ref[:, c * chunk:(c + 1) * chunk], vc, _NT,
                                    preferred_element_type=F32)
    for j in range(n_q):
        acc = acc + p_new[j] * vn[j:j + 1, :]
    out = jnp.where(own_head, acc / l, 0.0)
    out = jnp.sum(out.reshape(n_q, N_HEADS, D_ATTN), axis=1)
    o_ref[0] = out.astype(BF16)


def _moba_sample(page_table, q_bf, k_new, v_new, ck_t, cv_t, *, layer):
    bsz, n_pages = page_table.shape
    page = ck_t.shape[3]
    n_q = q_bf.shape[0] // bsz
    rows = n_q * N_HEADS
    pages_per_chunk = min(8, n_pages)
    chunk = pages_per_chunk * page
    past = n_pages * page
    assert n_pages % pages_per_chunk == 0 and chunk % MOBA_BLOCK == 0 and past // MOBA_BLOCK <= LANES
    blk3 = lambda b, pt: (b, 0, 0)
    kernel = functools.partial(_moba_sample_kernel, layer=layer, n_pages=n_pages, page=page,
                               pages_per_chunk=pages_per_chunk)
    out = pl.pallas_call(
        kernel,
        grid_spec=pltpu.PrefetchScalarGridSpec(
            num_scalar_prefetch=1,
            grid=(bsz,),
            in_specs=[pl.BlockSpec((1, n_q, D_ATTN), blk3),
                      pl.BlockSpec((1, n_q, D_ATTN), blk3),
                      pl.BlockSpec((1, n_q, D_ATTN), blk3),
                      pl.BlockSpec(memory_space=pl.ANY),
                      pl.BlockSpec(memory_space=pl.ANY)],
            out_specs=pl.BlockSpec((1, n_q, D_ATTN), blk3),
            scratch_shapes=[pltpu.VMEM((2, D_ATTN, chunk), F32),
                            pltpu.SemaphoreType.DMA((2,)),
                            pltpu.VMEM((rows, past), F32),
                            pltpu.VMEM((rows, past), BF16)]),
        out_shape=jax.ShapeDtypeStruct((bsz, n_q, D_ATTN), BF16),
        compiler_params=_params(("arbitrary",)),
        name="moba_sample",
    )(page_table, q_bf.reshape(bsz, n_q, D_ATTN), k_new.reshape(bsz, n_q, D_ATTN),
      v_new.reshape(bsz, n_q, D_ATTN), ck_t, cv_t)
    return out.reshape(bsz * n_q, D_ATTN)


def _lru_terms(xc, wa_ref, ba_ref, wx_ref, bx_ref, lam_ref):
    xcb = xc.astype(BF16)
    r = jax.nn.sigmoid(jnp.dot(xcb, wa_ref[...], preferred_element_type=F32) + ba_ref[...])
    gi = jax.nn.sigmoid(jnp.dot(xcb, wx_ref[...], preferred_element_type=F32) + bx_ref[...])
    nlam = -lam_ref[...]
    softplus = jnp.maximum(nlam, 0.0) + jnp.log1p(jnp.exp(-jnp.abs(nlam)))
    log_a = -LRU_C * r * softplus
    a = jnp.exp(log_a)
    bt = jnp.sqrt(1.0 - a * a) * (gi * xc)
    return a, bt


def _rec_prompt_kernel(xr_ref, gr_ref, cw_ref, cb_ref, wa_ref, ba_ref, wx_ref, bx_ref, lam_ref,
                       rec_ref, hlast_ref, tail_ref, h_ref):
    @pl.when(pl.program_id(1) == 0)
    def _():
        tail_ref[...] = jnp.zeros_like(tail_ref)
        h_ref[...] = jnp.zeros_like(h_ref)

    xr = xr_ref[...]
    t_rows = xr.shape[0]
    xp = jnp.concatenate([tail_ref[...], xr], axis=0)
    tail_ref[...] = xr[t_rows - SUBLANES:, :]
    cw = cw_ref[...]
    xc = cb_ref[...]
    for j in range(CONV_W):
        off = SUBLANES - (CONV_W - 1) + j
        xc = xc + xp[off:off + t_rows, :] * cw[j:j + 1, :]

    a, bt = _lru_terms(xc, wa_ref, ba_ref, wx_ref, bx_ref, lam_ref)

    sub = lax.broadcasted_iota(jnp.int32, a.shape, 0) % SUBLANES
    d = 1
    while d < SUBLANES:
        keep = sub >= d
        a_sh = pltpu.roll(a, d, axis=0)
        b_sh = pltpu.roll(bt, d, axis=0)
        bt = jnp.where(keep, a * b_sh + bt, bt)
        a = jnp.where(keep, a * a_sh, a)
        d *= 2
    h = h_ref[0:1, :]
    outs = []
    for g in range(t_rows // SUBLANES):
        hg = a[g * SUBLANES:(g + 1) * SUBLANES] * h + bt[g * SUBLANES:(g + 1) * SUBLANES]
        outs.append(hg)
        h = hg[SUBLANES - 1:SUBLANES, :]
    hs = jnp.concatenate(outs, axis=0)
    h_ref[...] = jnp.broadcast_to(h, h_ref.shape)
    hlast_ref[0] = h
    rec_ref[...] = (hs * jax.nn.gelu(gr_ref[...])).astype(BF16)


def _rec_prompt(xr, gr, cw, cb, wa_bd, ba, wx_bd, bx, lam, *, bsz, seq, tile):
    n = bsz * seq
    tps = seq // tile
    row = lambda b, j: (b * tps + j, 0)
    return pl.pallas_call(
        _rec_prompt_kernel,
        grid=(bsz, tps),
        in_specs=[pl.BlockSpec((tile, D_REC), row), pl.BlockSpec((tile, D_REC), row),
                  _resident((CONV_W, D_REC)), _resident((1, D_REC)),
                  _resident((D_REC, D_REC)), _resident((1, D_REC)),
                  _resident((D_REC, D_REC)), _resident((1, D_REC)), _resident((1, D_REC))],
        out_specs=[pl.BlockSpec((tile, D_REC), row),
                   pl.BlockSpec((1, 1, D_REC), lambda b, j: (b, 0, 0))],
        out_shape=[jax.ShapeDtypeStruct((n, D_REC), BF16),
                   jax.ShapeDtypeStruct((bsz, 1, D_REC), F32)],
        scratch_shapes=[pltpu.VMEM((SUBLANES, D_REC), F32), pltpu.VMEM((SUBLANES, D_REC), F32)],
        compiler_params=_params(("arbitrary", "arbitrary")),
        name="rec_prompt",
    )(xr, gr, cw, cb, wa_bd, ba, wx_bd, bx, lam)


def _rec_sample_kernel(xr_ref, gr_ref, prev_ref, h0_ref, cw_ref, cb_ref, wa_ref, ba_ref, wx_ref, bx_ref,
                       lam_ref, rec_ref, h_ref, *, steps):
    x = xr_ref[...]
    prev = prev_ref[...]
    n = x.shape[0]
    t = lax.broadcasted_iota(jnp.int32, x.shape, 0) % steps
    cw = cw_ref[...]
    xc = cb_ref[...] + x * cw[CONV_W - 1:CONV_W, :]
    for d in range(1, CONV_W):
        from_x = pltpu.roll(x, d, axis=0)
        from_prev = pltpu.roll(prev, (n - (CONV_W - 1 - d)) % n, axis=0)
        xc = xc + jnp.where(t >= d, from_x, from_prev) * cw[CONV_W - 1 - d:CONV_W - d, :]

    a, bt = _lru_terms(xc, wa_ref, ba_ref, wx_ref, bx_ref, lam_ref)
    d = 1
    while d < steps:
        keep = t >= d
        a_sh = pltpu.roll(a, d, axis=0)
        b_sh = pltpu.roll(bt, d, axis=0)
        bt = jnp.where(keep, a * b_sh + bt, bt)
        a = jnp.where(keep, a * a_sh, a)
        d *= 2
    h = a * h0_ref[...] + bt
    h_ref[...] = h
    rec_ref[...] = (h * jax.nn.gelu(gr_ref[...])).astype(BF16)


def _rec_sample(xr, gr, prev, h0_rep, cw, cb, wa_bd, ba, wx_bd, bx, lam, *, steps):
    n = xr.shape[0]
    full = lambda shape: pl.BlockSpec(shape, lambda i: (0,) * len(shape))
    return pl.pallas_call(
        functools.partial(_rec_sample_kernel, steps=steps),
        grid=(1,),
        in_specs=[full((n, D_REC))] * 4 + [full((CONV_W, D_REC)), full((1, D_REC)),
                                            full((D_REC, D_REC)), full((1, D_REC)),
                                            full((D_REC, D_REC)), full((1, D_REC)), full((1, D_REC))],
        out_specs=[full((n, D_REC)), full((n, D_REC))],
        out_shape=[jax.ShapeDtypeStruct((n, D_REC), BF16), jax.ShapeDtypeStruct((n, D_REC), F32)],
        compiler_params=_params(("arbitrary",)),
        name="rec_sample",
    )(xr, gr, prev, h0_rep, cw, cb, wa_bd, ba, wx_bd, bx, lam)


FF_CHUNKS = 2


def _out_ffn_kernel(x_ref, attn_ref, rec_ref, wo_ref, gf_ref, wg_ref, wu_ref, wd_ref, gl_ref, o_ref, *, final):
    mixed = jnp.concatenate([attn_ref[...], rec_ref[...]], axis=1)
    x1 = x_ref[...] + jnp.dot(mixed, wo_ref[...], preferred_element_type=F32)
    hn = _rmsnorm(x1, gf_ref[...]).astype(BF16)
    fc = D_FF // FF_CHUNKS
    y = x1
    for c in range(FF_CHUNKS):
        g = jnp.dot(hn, wg_ref[:, c * fc:(c + 1) * fc], preferred_element_type=F32)
        u = jnp.dot(hn, wu_ref[:, c * fc:(c + 1) * fc], preferred_element_type=F32)
        act = (jax.nn.silu(g) * u).astype(BF16)
        y = y + jnp.dot(act, wd_ref[c * fc:(c + 1) * fc, :], preferred_element_type=F32)
    if final:
        y = _rmsnorm(y, gl_ref[...])
    o_ref[...] = y


def _out_ffn(x, attn, rec, wo, gf, wg, wu, wd, gl, *, tm, final):
    n = x.shape[0]
    row = lambda i: (i, 0)
    return pl.pallas_call(
        functools.partial(_out_ffn_kernel, final=final),
        grid=(n // tm,),
        in_specs=[pl.BlockSpec((tm, D_MODEL), row), pl.BlockSpec((tm, D_ATTN), row),
                  pl.BlockSpec((tm, D_REC), row),
                  _resident((D_ATTN + D_REC, D_MODEL)), _resident((1, D_MODEL)),
                  _resident((D_MODEL, D_FF)), _resident((D_MODEL, D_FF)), _resident((D_FF, D_MODEL)),
                  _resident((1, D_MODEL))],
        out_specs=pl.BlockSpec((tm, D_MODEL), row),
        out_shape=jax.ShapeDtypeStruct((n, D_MODEL), F32),
        compiler_params=_params(("parallel",)),
        name="out_ffn",
    )(x, attn, rec, wo, gf, wg, wu, wd, gl)


def _block_diag(w):
    h, n, _ = w.shape
    eye = jnp.eye(h, dtype=w.dtype)
    return (eye[:, None, :, None] * w[:, :, None, :]).reshape(h * n, h * n)


def kernel(x_prompt, x_sample, cache_k, cache_v, state_h, state_conv, page_table, norm_mix, w_in, conv_w, conv_b, w_gate_a, b_gate_a, w_gate_x, b_gate_x, lru_lambda, w_out, norm_ffn, w_ff_gate, w_ff_up, w_ff_down, norm_final):
    bp, sp, _ = x_prompt.shape
    bs, ss, _ = x_sample.shape
    depth = w_in.shape[0]
    n_phys, page = cache_k.shape[1:3]
    past = page_table.shape[1] * page
    assert sp % MOBA_BLOCK == 0 and past % MOBA_BLOCK == 0 and ss >= CONV_W - 1

    tm_p = 512
    pos_p = jnp.arange(sp, dtype=jnp.int32)
    tab_p, tab_pt = _rope_table(pos_p), _rope_table_t(pos_p)
    tab_s = _rope_table(past + (jnp.arange(bs * ss, dtype=jnp.int32) % ss))

    ck_t = jnp.transpose(cache_k, (0, 1, 3, 4, 2)).reshape(depth, n_phys, D_ATTN, page)
    cv_t = jnp.transpose(cache_v, (0, 1, 3, 4, 2)).reshape(depth, n_phys, D_ATTN, page)

    row2 = lambda a: a.reshape(1, -1)
    xp = x_prompt.reshape(bp * sp, D_MODEL)
    xs = x_sample.reshape(bs * ss, D_MODEL)
    gl = row2(norm_final)
    kp_l, vp_l, hp_l, cp_l, ks_l, vs_l, hs_l, cs_l = [], [], [], [], [], [], [], []
    for l in range(depth):
        final = l == depth - 1
        w_in_bf = w_in[l].astype(BF16)
        wq = w_in_bf[:, :D_ATTN]
        wkv_t = w_in_bf[:, D_ATTN:3 * D_ATTN].T
        wr = w_in_bf[:, 3 * D_ATTN:]
        wo_bf = w_out[l].astype(BF16)
        wg_bf, wu_bf, wd_bf = w_ff_gate[l].astype(BF16), w_ff_up[l].astype(BF16), w_ff_down[l].astype(BF16)
        wa_bd = _block_diag(w_gate_a[l]).astype(BF16)
        wx_bd = _block_diag(w_gate_x[l]).astype(BF16)
        g_mix, g_ffn = row2(norm_mix[l]), row2(norm_ffn[l])
        rec_w = (conv_w[l], row2(conv_b[l]), wa_bd, row2(b_gate_a[l]), wx_bd, row2(b_gate_x[l]),
                 row2(lru_lambda[l]))

        q_bf, k_t, v_t, xr, gr = _proj_prompt(xp, g_mix, wq, wkv_t, wr, tab_p, tab_pt, bsz=bp, seq=sp, tm=tm_p)
        attn = _moba_prompt(q_bf, k_t, v_t, bsz=bp, seq=sp)
        rec, h_last = _rec_prompt(xr, gr, *rec_w, bsz=bp, seq=sp, tile=256)
        xp = _out_ffn(xp, attn, rec, wo_bf, g_ffn, wg_bf, wu_bf, wd_bf, gl, tm=tm_p, final=final)
        kp_l.append(k_t)
        vp_l.append(v_t)
        hp_l.append(h_last.reshape(bp, D_REC))
        cp_l.append(xr.reshape(bp, sp, D_REC)[:, sp - (CONV_W - 1):])

        q_bf, k, v, xr, gr = _proj_sample(xs, g_mix, w_in_bf, tab_s)
        attn = _moba_sample(page_table, q_bf, k, v, ck_t, cv_t, layer=l)
        prev = jnp.pad(state_conv[l], ((0, 0), (0, ss - (CONV_W - 1)), (0, 0))).reshape(bs * ss, D_REC)
        h0_rep = jnp.repeat(state_h[l], ss, axis=0)
        rec, h_all = _rec_sample(xr, gr, prev, h0_rep, *rec_w, steps=ss)
        xs = _out_ffn(xs, attn, rec, wo_bf, g_ffn, wg_bf, wu_bf, wd_bf, gl, tm=bs * ss, final=final)
        ks_l.append(k.reshape(bs, ss, N_HEADS, HEAD_DIM))
        vs_l.append(v.reshape(bs, ss, N_HEADS, HEAD_DIM))
        hs_l.append(h_all.reshape(bs, ss, D_REC)[:, ss - 1])
        cs_l.append(xr.reshape(bs, ss, D_REC)[:, ss - (CONV_W - 1):])

    def token_major(parts):
        t = jnp.stack(parts).reshape(depth, bp, N_HEADS, HEAD_DIM, sp)
        return jnp.transpose(t, (0, 1, 4, 2, 3))

    return (xp.reshape(bp, sp, D_MODEL), xs.reshape(bs, ss, D_MODEL),
            token_major(kp_l), token_major(vp_l), jnp.stack(hp_l), jnp.stack(cp_l),
            jnp.stack(ks_l), jnp.stack(vs_l), jnp.stack(hs_l), jnp.stack(cs_l))
```

```python
import functools

import jax
import jax.numpy as jnp
from jax import lax
from jax.experimental import pallas as pl
from jax.experimental.pallas import tpu as pltpu

D_MODEL = 1024
N_HEADS = 8
HEAD_DIM = 64
D_ATTN = N_HEADS * HEAD_DIM
N_REC_HEADS = 8
REC_BLOCK = 64
D_REC = N_REC_HEADS * REC_BLOCK
D_IN = 3 * D_ATTN + 2 * D_REC
CONV_W = 4
LRU_C = 8.0
MOBA_BLOCK = 256
MOBA_TOPK = 3
D_FF = 2816
ROPE_THETA = 10000.0
EPS = 1e-6
NEG = -1e30
LOG2E = 1.4426950408889634

LANES = 128
SUBLANES = 8
HEADS_PER_GROUP = LANES // HEAD_DIM
N_GROUPS = N_HEADS // HEADS_PER_GROUP
VMEM_LIMIT = 56 * 1024 * 1024

F32 = jnp.float32
BF16 = jnp.bfloat16
_NT = (((1,), (1,)), ((), ()))


def _params(semantics):
    return pltpu.CompilerParams(dimension_semantics=semantics, vmem_limit_bytes=VMEM_LIMIT)


def _resident(shape):
    zeros = (0,) * len(shape)
    return pl.BlockSpec(shape, lambda *_: zeros, pipeline_mode=pl.Buffered(1))


def _rmsnorm(x, g):
    y = x * lax.rsqrt(jnp.mean(x * x, axis=-1, keepdims=True) + EPS)
    return y * g


def _rope_angles(pos):
    half = HEAD_DIM // 2
    inv = ROPE_THETA ** (-jnp.arange(half, dtype=F32) / half)
    ang = pos.astype(F32)[:, None] * inv[None, :]
    return jnp.cos(ang), jnp.sin(ang)


def _rope_table(pos):
    cos, sin = _rope_angles(pos)
    zero = jnp.zeros_like(sin)
    reps = LANES // HEAD_DIM
    cos_t = jnp.tile(cos, (1, 2 * reps))
    sin_a = jnp.tile(jnp.concatenate([-sin, zero], axis=1), (1, reps))
    sin_b = jnp.tile(jnp.concatenate([zero, sin], axis=1), (1, reps))
    return jnp.concatenate([cos_t, sin_a, sin_b], axis=1)


def _rope_table_t(pos):
    cos, sin = _rope_angles(pos)
    return jnp.concatenate([cos, cos, -sin, sin], axis=1).T


def _rope_lanes(t, tab):
    cos, sin_a, sin_b = tab[:, :LANES], tab[:, LANES:2 * LANES], tab[:, 2 * LANES:]
    half = HEAD_DIM // 2
    pieces = []
    for j in range(D_ATTN // LANES):
        tj = t[:, j * LANES:(j + 1) * LANES]
        pieces.append(tj * cos + pltpu.roll(tj, LANES - half, axis=1) * sin_a
                      + pltpu.roll(tj, half, axis=1) * sin_b)
    return jnp.concatenate(pieces, axis=1)


def _rope_rows(t, tab_t):
    cos, sin = tab_t[:HEAD_DIM], tab_t[HEAD_DIM:]
    half = HEAD_DIM // 2
    pieces = []
    for h in range(N_HEADS):
        th = t[h * HEAD_DIM:(h + 1) * HEAD_DIM]
        partner = jnp.concatenate([th[half:], th[:half]], axis=0)
        pieces.append(th * cos + partner * sin)
    return jnp.concatenate(pieces, axis=0)


def _proj_prompt_kernel(x_ref, g_ref, wq_ref, wkv_ref, wr_ref, tab_ref, tabt_ref, *refs):
    q_ref, kt_ref, vt_ref, xr_ref, gr_ref = refs[-5:]
    hn = _rmsnorm(x_ref[...], g_ref[...]).astype(BF16)
    q = jnp.dot(hn, wq_ref[...], preferred_element_type=F32)
    q_ref[...] = (_rope_lanes(q, tab_ref[...]) * (HEAD_DIM ** -0.5 * LOG2E)).astype(BF16)
    kv_t = lax.dot_general(wkv_ref[...], hn, _NT, preferred_element_type=F32)
    kt_ref[0, 0] = _rope_rows(kv_t[:D_ATTN], tabt_ref[...])
    vt_ref[0, 0] = kv_t[D_ATTN:]
    r = jnp.dot(hn, wr_ref[...], preferred_element_type=F32)
    xr_ref[...] = r[:, :D_REC]
    gr_ref[...] = r[:, D_REC:]


def _proj_prompt(x, g, wq, wkv_t, wr, tab, tab_t, kv_prev, *, bsz, seq, tm, layer, depth):
    n = bsz * seq
    tps = seq // tm
    row = lambda b, j: (b * tps + j, 0)
    col = lambda b, j: (layer, b, 0, j)
    in_specs = [pl.BlockSpec((tm, D_MODEL), row),
                _resident((1, D_MODEL)),
                _resident((D_MODEL, D_ATTN)),
                _resident((2 * D_ATTN, D_MODEL)),
                _resident((D_MODEL, 2 * D_REC)),
                pl.BlockSpec((tm, 3 * LANES), lambda b, j: (j, 0)),
                pl.BlockSpec((2 * HEAD_DIM, tm), lambda b, j: (0, j))]
    args = [x, g, wq, wkv_t, wr, tab, tab_t]
    aliases = {len(args): 1, len(args) + 1: 2}
    in_specs += [pl.BlockSpec(memory_space=pl.ANY)] * 2
    args += list(kv_prev)
    return pl.pallas_call(
        _proj_prompt_kernel,
        grid=(bsz, tps),
        in_specs=in_specs,
        out_specs=[pl.BlockSpec((tm, D_ATTN), row),
                   pl.BlockSpec((1, 1, D_ATTN, tm), col), pl.BlockSpec((1, 1, D_ATTN, tm), col),
                   pl.BlockSpec((tm, D_REC), row), pl.BlockSpec((tm, D_REC), row)],
        out_shape=[jax.ShapeDtypeStruct((n, D_ATTN), BF16),
                   jax.ShapeDtypeStruct((depth, bsz, D_ATTN, seq), F32),
                   jax.ShapeDtypeStruct((depth, bsz, D_ATTN, seq), F32),
                   jax.ShapeDtypeStruct((n, D_REC), F32), jax.ShapeDtypeStruct((n, D_REC), F32)],
        input_output_aliases=aliases,
        compiler_params=_params(("parallel", "parallel")),
        name="proj_prompt",
    )(*args)


def _proj_sample_kernel(x_ref, g_ref, w_ref, tab_ref, q_ref, k_ref, v_ref, xr_ref, gr_ref):
    hn = _rmsnorm(x_ref[...], g_ref[...]).astype(BF16)
    proj = jnp.dot(hn, w_ref[...], preferred_element_type=F32)
    tab = tab_ref[...]
    q_ref[...] = (_rope_lanes(proj[:, :D_ATTN], tab) * (HEAD_DIM ** -0.5)).astype(BF16)
    k_ref[...] = _rope_lanes(proj[:, D_ATTN:2 * D_ATTN], tab)
    v_ref[...] = proj[:, 2 * D_ATTN:3 * D_ATTN]
    xr_ref[...] = proj[:, 3 * D_ATTN:3 * D_ATTN + D_REC]
    gr_ref[...] = proj[:, 3 * D_ATTN + D_REC:]


def _proj_sample(x, g, w_bf, tab):
    n = x.shape[0]
    full = lambda shape: pl.BlockSpec(shape, lambda i: (0,) * len(shape))
    return pl.pallas_call(
        _proj_sample_kernel,
        grid=(1,),
        in_specs=[full((n, D_MODEL)), full((1, D_MODEL)), full((D_MODEL, D_IN)), full((n, 3 * LANES))],
        out_specs=[full((n, D_ATTN))] * 3 + [full((n, D_REC))] * 2,
        out_shape=[jax.ShapeDtypeStruct((n, D_ATTN), BF16),
                   jax.ShapeDtypeStruct((n, D_ATTN), F32), jax.ShapeDtypeStruct((n, D_ATTN), F32),
                   jax.ShapeDtypeStruct((n, D_REC), F32), jax.ShapeDtypeStruct((n, D_REC), F32)],
        compiler_params=_params(("arbitrary",)),
        name="proj_sample",
    )(x, g, w_bf, tab)


def _top3_bias(scores, idx_f, n_valid, axis):
    sg = jnp.where(idx_f < n_valid, scores, NEG)
    bias = jnp.full(sg.shape, NEG, F32)
    for _ in range(MOBA_TOPK):
        m = jnp.max(sg, axis=axis, keepdims=True)
        idx = jnp.min(jnp.where(sg == m, idx_f, float(LANES)), axis=axis, keepdims=True)
        pick = idx_f == idx
        bias = jnp.where(pick & (m > 0.5 * NEG), 0.0, bias)
        sg = jnp.where(pick, NEG, sg)
    return bias


def _lane_tiles(x):
    return [x[:, j * LANES:(j + 1) * LANES] for j in range(x.shape[1] // LANES)]


def _tree(op, xs):
    while len(xs) > 1:
        xs = [op(xs[i], xs[i + 1]) for i in range(0, len(xs) - 1, 2)] + ([xs[-1]] if len(xs) % 2 else [])
    return xs[0]


def _moba_prompt_kernel(q_ref, kt_ref, vt_ref, o_ref, kb_ref, vb_ref, km_ref,
                        s0_ref, s1_ref, mg0_ref, mg1_ref, m_ref, l_ref, acc_ref):
    own = pl.program_id(2)
    blk = MOBA_BLOCK
    n_groups, _, gk = kb_ref.shape
    gb = gk // blk
    nblk = n_groups * gb
    nb8 = -(-nblk // SUBLANES) * SUBLANES

    @pl.when(own == 0)
    def _():
        lane_k = lax.broadcasted_iota(jnp.int32, (LANES, LANES), 1)
        sub = lax.broadcasted_iota(jnp.int32, (LANES, gk), 0)
        blk_of_lane = lax.broadcasted_iota(jnp.int32, (LANES, gk), 1) // blk
        km = jnp.zeros((LANES, LANES), F32)
        for g in range(n_groups):
            kg = kt_ref[0, 0, :, g * gk:(g + 1) * gk]
            kb_ref[g, :LANES, :] = kg.astype(BF16)
            kb_ref[g, LANES:, :] = jnp.where(sub == g * gb + blk_of_lane, 1.0, 0.0).astype(BF16)
            vb_ref[g] = vt_ref[0, 0, :, g * gk:(g + 1) * gk].astype(BF16)
            for j in range(gb):
                mean = jnp.mean(kg[:, j * blk:(j + 1) * blk], axis=1, keepdims=True)
                km = jnp.where(lane_k == g * gb + j, mean, km)
        km_ref[...] = km.T

    q2 = q_ref[...]
    lane = lax.broadcasted_iota(jnp.int32, (blk, LANES), 1)
    cand_f = lax.broadcasted_iota(jnp.int32, (nb8, blk), 0).astype(F32)
    own_f = own.astype(F32)
    g_last = own // gb
    key_pos = g_last * gk + lax.broadcasted_iota(jnp.int32, (blk, gk), 1)
    q_pos = own * blk + lax.broadcasted_iota(jnp.int32, (blk, gk), 0)
    causal = key_pos <= q_pos

    heads = range(HEADS_PER_GROUP)
    in_head = [(lane >= h * HEAD_DIM) & (lane < (h + 1) * HEAD_DIM) for h in heads]
    q_aug = []
    for h in heads:
        qh = jnp.where(in_head[h], q2, jnp.zeros_like(q2))
        gate_t = lax.dot_general(km_ref[:nb8, :], qh.astype(F32), _NT, precision=lax.Precision.HIGHEST,
                                 preferred_element_type=F32)
        bias_t = _top3_bias(gate_t, cand_f, own_f, axis=0)
        bias_t = jnp.where(cand_f == own_f, 0.0, bias_t)
        bias = jnp.concatenate([bias_t, jnp.zeros((LANES - nb8, blk), F32)], axis=0).T
        q_aug.append(jnp.concatenate([qh, bias.astype(BF16)], axis=1))

    def logits(h, g):
        return jnp.dot(q_aug[h], kb_ref[g], preferred_element_type=F32)

    def stage_a(g, s_buf, mg_buf, masked):
        for h in heads:
            s = logits(h, g)
            if masked:
                s = jnp.where(causal, s, NEG)
            s_buf[h] = s
            mg_buf[h] = _tree(jnp.maximum, _lane_tiles(s))

    def stage_b(s_buf, mg_buf, g):
        vg = vb_ref[g]
        for h in heads:
            m_old = m_ref[h]
            m_new = jnp.maximum(m_old, jnp.broadcast_to(jnp.max(mg_buf[h], axis=1, keepdims=True),
                                                        (blk, LANES)))
            alpha = jnp.exp2(m_old - m_new)
            p = jnp.exp2(s_buf[h] - jnp.concatenate([m_new] * (gk // LANES), axis=1))
            m_ref[h] = m_new
            l_ref[h] = alpha * l_ref[h] + _tree(jnp.add, _lane_tiles(p))
            acc_ref[h] = alpha * acc_ref[h] + lax.dot_general(p.astype(BF16), vg, _NT,
                                                              preferred_element_type=F32)

    m_ref[...] = jnp.full(m_ref.shape, NEG, F32)
    l_ref[...] = jnp.zeros(l_ref.shape, F32)
    acc_ref[...] = jnp.zeros(acc_ref.shape, F32)

    stage_a(g_last, s0_ref, mg0_ref, True)
    n_pairs = g_last // 2

    def pair_body(j, _):
        stage_b(s0_ref, mg0_ref, jnp.where(j == 0, g_last, 2 * j - 1))
        stage_a(2 * j, s1_ref, mg1_ref, False)
        stage_b(s1_ref, mg1_ref, 2 * j)
        stage_a(2 * j + 1, s0_ref, mg0_ref, False)
        return 0

    lax.fori_loop(0, n_pairs, pair_body, 0)
    g_buffered = jnp.where(n_pairs == 0, g_last, 2 * n_pairs - 1)

    @pl.when(g_last % 2 == 1)
    def _():
        stage_b(s0_ref, mg0_ref, g_buffered)
        stage_a(g_last - 1, s1_ref, mg1_ref, False)
        stage_b(s1_ref, mg1_ref, g_last - 1)

    @pl.when(g_last % 2 == 0)
    def _():
        stage_b(s0_ref, mg0_ref, g_buffered)

    out = None
    for h in heads:
        oh = acc_ref[h] / jnp.sum(l_ref[h], axis=1, keepdims=True)
        out = oh if out is None else jnp.where(in_head[h], oh, out)
    o_ref[...] = out.astype(BF16)


GROUP_BLOCKS = 4


def _moba_prompt(q_bf, k_t, v_t, *, bsz, seq, layer):
    nblk = seq // MOBA_BLOCK
    n = bsz * seq
    gb = min(GROUP_BLOCKS, nblk)
    gk = gb * MOBA_BLOCK
    assert nblk <= LANES and nblk % gb == 0
    return pl.pallas_call(
        _moba_prompt_kernel,
        grid=(bsz, N_GROUPS, nblk),
        in_specs=[pl.BlockSpec((MOBA_BLOCK, LANES), lambda b, g, i: (b * nblk + i, g)),
                  pl.BlockSpec((1, 1, LANES, seq), lambda b, g, i: (layer, b, g, 0)),
                  pl.BlockSpec((1, 1, LANES, seq), lambda b, g, i: (layer, b, g, 0))],
        out_specs=pl.BlockSpec((MOBA_BLOCK, LANES), lambda b, g, i: (b * nblk + i, g)),
        out_shape=jax.ShapeDtypeStruct((n, D_ATTN), BF16),
        scratch_shapes=[pltpu.VMEM((nblk // gb, 2 * LANES, gk), BF16),
                        pltpu.VMEM((nblk // gb, LANES, gk), BF16),
                        pltpu.VMEM((LANES, LANES), F32)]
        + [pltpu.VMEM((HEADS_PER_GROUP, MOBA_BLOCK, gk), F32)] * 2
        + [pltpu.VMEM((HEADS_PER_GROUP, MOBA_BLOCK, LANES), F32)] * 5,
        compiler_params=_params(("arbitrary", "arbitrary", "arbitrary")),
        name="moba_prompt",
    )(q_bf, k_t, v_t)


def _moba_sample_kernel(pt_ref, q_ref, kn_ref, vn_ref, ck_hbm, cv_hbm, o_ref,
                        buf, sem, s_ref, p_ref, *, layer, n_pages, page, pages_per_chunk):
    b = pl.program_id(0)
    n_q = q_ref.shape[1]
    rows = n_q * N_HEADS
    chunk = pages_per_chunk * page
    n_chunks = n_pages // pages_per_chunk
    blocks_per_chunk = chunk // MOBA_BLOCK
    nblk = n_pages * page // MOBA_BLOCK

    n_slots = buf.shape[0]
    ahead = n_slots - 1
    n_loads = 2 * n_chunks
    assert n_loads % n_slots == 0 and ahead <= n_loads
    loads = [(ck_hbm, c) for c in range(n_chunks)] + [(cv_hbm, c) for c in range(n_chunks)]

    def chunk_copies(row, i):
        src, c = loads[i]
        return [pltpu.make_async_copy(src.at[layer, pt_ref[row, c * pages_per_chunk + j]],
                                      buf.at[i % n_slots, j], sem.at[i % n_slots])
                for j in range(pages_per_chunk)]

    def start(row, i):
        for cp in chunk_copies(row, i):
            cp.start()

    def start_ahead(i):
        nxt = i + ahead
        if nxt < n_loads:
            start(b, nxt)
        else:
            @pl.when(b + 1 < pl.num_programs(0))
            def _():
                start(b + 1, nxt - n_loads)

    def wait(i):
        for cp in chunk_copies(b, i):
            cp.wait()

    def chunk_value(i):
        return jnp.concatenate([buf[i % n_slots, j] for j in range(pages_per_chunk)], axis=1).astype(BF16)

    @pl.when(b == 0)
    def _():
        for i in range(ahead):
            start(b, i)

    q4 = q_ref[0]
    row_i = lax.broadcasted_iota(jnp.int32, (rows, D_ATTN), 0)
    col_i = lax.broadcasted_iota(jnp.int32, (rows, D_ATTN), 1)
    own_head = (col_i // HEAD_DIM) == (row_i % N_HEADS)
    q_rep = jnp.concatenate([jnp.broadcast_to(q4[t:t + 1], (N_HEADS, D_ATTN)) for t in range(n_q)], axis=0)
    q_bd = jnp.where(own_head, q_rep, jnp.zeros_like(q_rep))

    lane = lax.broadcasted_iota(jnp.int32, (rows, LANES), 1)
    gate = jnp.zeros((rows, LANES), F32)
    for i in range(n_chunks):
        wait(i)
        start_ahead(i)
        kc = chunk_value(i)
        s_c = jnp.dot(q_bd, kc, preferred_element_type=F32)
        s_ref[:, i * chunk:(i + 1) * chunk] = s_c
        for j in range(blocks_per_chunk):
            tot = jnp.sum(s_c[:, j * MOBA_BLOCK:(j + 1) * MOBA_BLOCK], axis=1, keepdims=True)
            gate = jnp.where(lane == i * blocks_per_chunk + j, tot * (1.0 / MOBA_BLOCK), gate)

    bias = _top3_bias(gate, lane.astype(F32), float(nblk), axis=1)

    tok = lax.broadcasted_iota(jnp.int32, (rows, 1), 0) // N_HEADS
    q_bd_f = q_bd.astype(F32)
    kn = kn_ref[0]
    vn = vn_ref[0]
    s_new = []
    for j in range(n_q):
        sj = jnp.sum(q_bd_f * kn[j:j + 1, :], axis=1, keepdims=True)
        s_new.append(jnp.where(tok >= j, sj, NEG))

    m = s_new[0]
    for sj in s_new[1:]:
        m = jnp.maximum(m, sj)
    bias_cols = []
    for n in range(nblk):
        bcol = jnp.max(jnp.where(lane == n, bias, NEG), axis=1, keepdims=True)
        bias_cols.append(bcol)
        sb = s_ref[:, n * MOBA_BLOCK:(n + 1) * MOBA_BLOCK] + bcol
        m = jnp.maximum(m, jnp.max(sb, axis=1, keepdims=True))
    p_new = [jnp.exp(sj - m) for sj in s_new]
    l = p_new[0]
    for pj in p_new[1:]:
        l = l + pj
    for n in range(nblk):
        pb = jnp.exp(s_ref[:, n * MOBA_BLOCK:(n + 1) * MOBA_BLOCK] + bias_cols[n] - m)
        l = l + jnp.sum(pb, axis=1, keepdims=True)
        p_ref[:, n * MOBA_BLOCK:(n + 1) * MOBA_BLOCK] = pb.astype(BF16)

    acc = jnp.zeros((rows, D_ATTN), F32)
    for i in range(n_chunks, 2 * n_chunks):
        wait(i)
        start_ahead(i)
        c = i - n_chunks
        vc = chunk_value(i)
        acc = acc + lax.dot_general(p_ref[:, c * chunk:(c + 1) * chunk], vc, _NT,
                                    preferred_element_type=F32)
    for j in range(n_q):
        acc = acc + p_new[j] * vn[j:j + 1, :]
    out = jnp.where(own_head, acc / l, 0.0)
    out = jnp.sum(out.reshape(n_q, N_HEADS, D_ATTN), axis=1)
    o_ref[0] = out.astype(BF16)


SAMPLE_SLOTS = 4


def _moba_sample(page_table, q_bf, k_new, v_new, ck_t, cv_t, *, layer):
    bsz, n_pages = page_table.shape
    page = ck_t.shape[3]
    n_q = q_bf.shape[0] // bsz
    rows = n_q * N_HEADS
    pages_per_chunk = min(8, n_pages)
    chunk = pages_per_chunk * page
    n_slots = min(SAMPLE_SLOTS, 2 * n_pages // pages_per_chunk)
    past = n_pages * page
    assert n_pages % pages_per_chunk == 0 and chunk % MOBA_BLOCK == 0 and past // MOBA_BLOCK <= LANES
    blk3 = lambda b, pt: (b, 0, 0)
    kernel = functools.partial(_moba_sample_kernel, layer=layer, n_pages=n_pages, page=page,
                               pages_per_chunk=pages_per_chunk)
    out = pl.pallas_call(
        kernel,
        grid_spec=pltpu.PrefetchScalarGridSpec(
            num_scalar_prefetch=1,
            grid=(bsz,),
            in_specs=[pl.BlockSpec((1, n_q, D_ATTN), blk3),
                      pl.BlockSpec((1, n_q, D_ATTN), blk3),
                      pl.BlockSpec((1, n_q, D_ATTN), blk3),
                      pl.BlockSpec(memory_space=pl.ANY),
                      pl.BlockSpec(memory_space=pl.ANY)],
            out_specs=pl.BlockSpec((1, n_q, D_ATTN), blk3),
            scratch_shapes=[pltpu.VMEM((n_slots, pages_per_chunk, D_ATTN, page), F32),
                            pltpu.SemaphoreType.DMA((n_slots,)),
                            pltpu.VMEM((rows, past), F32),
                            pltpu.VMEM((rows, past), BF16)]),
        out_shape=jax.ShapeDtypeStruct((bsz, n_q, D_ATTN), BF16),
        compiler_params=_params(("arbitrary",)),
        name="moba_sample",
    )(page_table, q_bf.reshape(bsz, n_q, D_ATTN), k_new.reshape(bsz, n_q, D_ATTN),
      v_new.reshape(bsz, n_q, D_ATTN), ck_t, cv_t)
    return out.reshape(bsz * n_q, D_ATTN)


def _lru_terms(xc, wa_ref, ba_ref, wx_ref, bx_ref, lam_ref):
    xcb = xc.astype(BF16)
    r = jax.nn.sigmoid(jnp.dot(xcb, wa_ref[...], preferred_element_type=F32) + ba_ref[...])
    gi = jax.nn.sigmoid(jnp.dot(xcb, wx_ref[...], preferred_element_type=F32) + bx_ref[...])
    nlam = -lam_ref[...]
    softplus = jnp.maximum(nlam, 0.0) + jnp.log1p(jnp.exp(-jnp.abs(nlam)))
    log_a = -LRU_C * r * softplus
    a = jnp.exp(log_a)
    bt = jnp.sqrt(1.0 - a * a) * (gi * xc)
    return a, bt


def _rec_prompt_kernel(xr_ref, gr_ref, cw_ref, cb_ref, wa_ref, ba_ref, wx_ref, bx_ref, lam_ref,
                       rec_ref, hlast_ref, tail_ref, h_ref):
    @pl.when(pl.program_id(1) == 0)
    def _():
        tail_ref[...] = jnp.zeros_like(tail_ref)
        h_ref[...] = jnp.zeros_like(h_ref)

    xr = xr_ref[...]
    t_rows = xr.shape[0]
    xp = jnp.concatenate([tail_ref[...], xr], axis=0)
    tail_ref[...] = xr[t_rows - SUBLANES:, :]
    cw = cw_ref[...]
    xc = cb_ref[...]
    for j in range(CONV_W):
        off = SUBLANES - (CONV_W - 1) + j
        xc = xc + xp[off:off + t_rows, :] * cw[j:j + 1, :]

    a, bt = _lru_terms(xc, wa_ref, ba_ref, wx_ref, bx_ref, lam_ref)

    sub = lax.broadcasted_iota(jnp.int32, a.shape, 0) % SUBLANES
    d = 1
    while d < SUBLANES:
        keep = sub >= d
        a_sh = pltpu.roll(a, d, axis=0)
        b_sh = pltpu.roll(bt, d, axis=0)
        bt = jnp.where(keep, a * b_sh + bt, bt)
        a = jnp.where(keep, a * a_sh, a)
        d *= 2
    h = h_ref[0:1, :]
    outs = []
    for g in range(t_rows // SUBLANES):
        hg = a[g * SUBLANES:(g + 1) * SUBLANES] * h + bt[g * SUBLANES:(g + 1) * SUBLANES]
        outs.append(hg)
        h = hg[SUBLANES - 1:SUBLANES, :]
    hs = jnp.concatenate(outs, axis=0)
    h_ref[...] = jnp.broadcast_to(h, h_ref.shape)
    hlast_ref[0] = h
    rec_ref[...] = (hs * jax.nn.gelu(gr_ref[...])).astype(BF16)


def _rec_prompt(xr, gr, cw, cb, wa_bd, ba, wx_bd, bx, lam, *, bsz, seq, tile):
    n = bsz * seq
    tps = seq // tile
    row = lambda b, j: (b * tps + j, 0)
    return pl.pallas_call(
        _rec_prompt_kernel,
        grid=(bsz, tps),
        in_specs=[pl.BlockSpec((tile, D_REC), row), pl.BlockSpec((tile, D_REC), row),
                  _resident((CONV_W, D_REC)), _resident((1, D_REC)),
                  _resident((D_REC, D_REC)), _resident((1, D_REC)),
                  _resident((D_REC, D_REC)), _resident((1, D_REC)), _resident((1, D_REC))],
        out_specs=[pl.BlockSpec((tile, D_REC), row),
                   pl.BlockSpec((1, 1, D_REC), lambda b, j: (b, 0, 0))],
        out_shape=[jax.ShapeDtypeStruct((n, D_REC), BF16),
                   jax.ShapeDtypeStruct((bsz, 1, D_REC), F32)],
        scratch_shapes=[pltpu.VMEM((SUBLANES, D_REC), F32), pltpu.VMEM((SUBLANES, D_REC), F32)],
        compiler_params=_params(("arbitrary", "arbitrary")),
        name="rec_prompt",
    )(xr, gr, cw, cb, wa_bd, ba, wx_bd, bx, lam)


def _rec_sample_kernel(xr_ref, gr_ref, prev_ref, h0_ref, cw_ref, cb_ref, wa_ref, ba_ref, wx_ref, bx_ref,
                       lam_ref, rec_ref, h_ref, *, steps):
    x = xr_ref[...]
    prev = prev_ref[...]
    n = x.shape[0]
    t = lax.broadcasted_iota(jnp.int32, x.shape, 0) % steps
    cw = cw_ref[...]
    xc = cb_ref[...] + x * cw[CONV_W - 1:CONV_W, :]
    for d in range(1, CONV_W):
        from_x = pltpu.roll(x, d, axis=0)
        from_prev = pltpu.roll(prev, (n - (CONV_W - 1 - d)) % n, axis=0)
        xc = xc + jnp.where(t >= d, from_x, from_prev) * cw[CONV_W - 1 - d:CONV_W - d, :]

    a, bt = _lru_terms(xc, wa_ref, ba_ref, wx_ref, bx_ref, lam_ref)
    d = 1
    while d < steps:
        keep = t >= d
        a_sh = pltpu.roll(a, d, axis=0)
        b_sh = pltpu.roll(bt, d, axis=0)
        bt = jnp.where(keep, a * b_sh + bt, bt)
        a = jnp.where(keep, a * a_sh, a)
        d *= 2
    h = a * h0_ref[...] + bt
    h_ref[...] = h
    rec_ref[...] = (h * jax.nn.gelu(gr_ref[...])).astype(BF16)


def _rec_sample(xr, gr, prev, h0_rep, cw, cb, wa_bd, ba, wx_bd, bx, lam, *, steps):
    n = xr.shape[0]
    full = lambda shape: pl.BlockSpec(shape, lambda i: (0,) * len(shape))
    return pl.pallas_call(
        functools.partial(_rec_sample_kernel, steps=steps),
        grid=(1,),
        in_specs=[full((n, D_REC))] * 4 + [full((CONV_W, D_REC)), full((1, D_REC)),
                                            full((D_REC, D_REC)), full((1, D_REC)),
                                            full((D_REC, D_REC)), full((1, D_REC)), full((1, D_REC))],
        out_specs=[full((n, D_REC)), full((n, D_REC))],
        out_shape=[jax.ShapeDtypeStruct((n, D_REC), BF16), jax.ShapeDtypeStruct((n, D_REC), F32)],
        compiler_params=_params(("arbitrary",)),
        name="rec_sample",
    )(xr, gr, prev, h0_rep, cw, cb, wa_bd, ba, wx_bd, bx, lam)


FF_CHUNKS = 2


def _out_ffn_kernel(x_ref, attn_ref, rec_ref, wo_ref, gf_ref, wg_ref, wu_ref, wd_ref, gl_ref, o_ref, *, final):
    mixed = jnp.concatenate([attn_ref[...], rec_ref[...]], axis=1)
    x1 = x_ref[...] + jnp.dot(mixed, wo_ref[...], preferred_element_type=F32)
    hn = _rmsnorm(x1, gf_ref[...]).astype(BF16)
    fc = D_FF // FF_CHUNKS
    y = x1
    for c in range(FF_CHUNKS):
        g = jnp.dot(hn, wg_ref[:, c * fc:(c + 1) * fc], preferred_element_type=F32)
        u = jnp.dot(hn, wu_ref[:, c * fc:(c + 1) * fc], preferred_element_type=F32)
        act = (jax.nn.silu(g) * u).astype(BF16)
        y = y + jnp.dot(act, wd_ref[c * fc:(c + 1) * fc, :], preferred_element_type=F32)
    if final:
        y = _rmsnorm(y, gl_ref[...])
    o_ref[...] = y


def _out_ffn(x, attn, rec, wo, gf, wg, wu, wd, gl, *, tm, final):
    n = x.shape[0]
    row = lambda i: (i, 0)
    return pl.pallas_call(
        functools.partial(_out_ffn_kernel, final=final),
        grid=(n // tm,),
        in_specs=[pl.BlockSpec((tm, D_MODEL), row), pl.BlockSpec((tm, D_ATTN), row),
                  pl.BlockSpec((tm, D_REC), row),
                  _resident((D_ATTN + D_REC, D_MODEL)), _resident((1, D_MODEL)),
                  _resident((D_MODEL, D_FF)), _resident((D_MODEL, D_FF)), _resident((D_FF, D_MODEL)),
                  _resident((1, D_MODEL))],
        out_specs=pl.BlockSpec((tm, D_MODEL), row),
        out_shape=jax.ShapeDtypeStruct((n, D_MODEL), F32),
        compiler_params=_params(("parallel",)),
        name="out_ffn",
    )(x, attn, rec, wo, gf, wg, wu, wd, gl)


def _block_diag(w):
    h, n, _ = w.shape
    eye = jnp.eye(h, dtype=w.dtype)
    return (eye[:, None, :, None] * w[:, :, None, :]).reshape(h * n, h * n)


def kernel(x_prompt, x_sample, cache_k, cache_v, state_h, state_conv, page_table, norm_mix, w_in, conv_w, conv_b, w_gate_a, b_gate_a, w_gate_x, b_gate_x, lru_lambda, w_out, norm_ffn, w_ff_gate, w_ff_up, w_ff_down, norm_final):
    bp, sp, _ = x_prompt.shape
    bs, ss, _ = x_sample.shape
    depth = w_in.shape[0]
    n_phys, page = cache_k.shape[1:3]
    past = page_table.shape[1] * page
    assert sp % MOBA_BLOCK == 0 and past % MOBA_BLOCK == 0 and ss >= CONV_W - 1

    tm_p = 512
    pos_p = jnp.arange(sp, dtype=jnp.int32)
    tab_p, tab_pt = _rope_table(pos_p), _rope_table_t(pos_p)
    tab_s = _rope_table(past + (jnp.arange(bs * ss, dtype=jnp.int32) % ss))

    ck_t = jnp.transpose(cache_k, (0, 1, 3, 4, 2)).reshape(depth, n_phys, D_ATTN, page)
    cv_t = jnp.transpose(cache_v, (0, 1, 3, 4, 2)).reshape(depth, n_phys, D_ATTN, page)

    row2 = lambda a: a.reshape(1, -1)
    xp = x_prompt.reshape(bp * sp, D_MODEL)
    xs = x_sample.reshape(bs * ss, D_MODEL)
    gl = row2(norm_final)
    hp_l, cp_l, ks_l, vs_l, hs_l, cs_l = [], [], [], [], [], []
    kv_t = (jnp.zeros((depth, bp, D_ATTN, sp), F32), jnp.zeros((depth, bp, D_ATTN, sp), F32))
    for l in range(depth):
        final = l == depth - 1
        w_in_bf = w_in[l].astype(BF16)
        wq = w_in_bf[:, :D_ATTN]
        wkv_t = w_in_bf[:, D_ATTN:3 * D_ATTN].T
        wr = w_in_bf[:, 3 * D_ATTN:]
        wo_bf = w_out[l].astype(BF16)
        wg_bf, wu_bf, wd_bf = w_ff_gate[l].astype(BF16), w_ff_up[l].astype(BF16), w_ff_down[l].astype(BF16)
        wa_bd = _block_diag(w_gate_a[l]).astype(BF16)
        wx_bd = _block_diag(w_gate_x[l]).astype(BF16)
        g_mix, g_ffn = row2(norm_mix[l]), row2(norm_ffn[l])
        rec_w = (conv_w[l], row2(conv_b[l]), wa_bd, row2(b_gate_a[l]), wx_bd, row2(b_gate_x[l]),
                 row2(lru_lambda[l]))

        q_bf, k_t, v_t, xr, gr = _proj_prompt(xp, g_mix, wq, wkv_t, wr, tab_p, tab_pt, kv_t, bsz=bp, seq=sp,
                                              tm=tm_p, layer=l, depth=depth)
        kv_t = (k_t, v_t)
        attn = _moba_prompt(q_bf, k_t, v_t, bsz=bp, seq=sp, layer=l)
        rec, h_last = _rec_prompt(xr, gr, *rec_w, bsz=bp, seq=sp, tile=256)
        xp = _out_ffn(xp, attn, rec, wo_bf, g_ffn, wg_bf, wu_bf, wd_bf, gl, tm=tm_p, final=final)
        hp_l.append(h_last.reshape(bp, D_REC))
        cp_l.append(xr.reshape(bp, sp, D_REC)[:, sp - (CONV_W - 1):])

        q_bf, k, v, xr, gr = _proj_sample(xs, g_mix, w_in_bf, tab_s)
        attn = _moba_sample(page_table, q_bf, k, v, ck_t, cv_t, layer=l)
        prev = jnp.pad(state_conv[l], ((0, 0), (0, ss - (CONV_W - 1)), (0, 0))).reshape(bs * ss, D_REC)
        h0_rep = jnp.repeat(state_h[l], ss, axis=0)
        rec, h_all = _rec_sample(xr, gr, prev, h0_rep, *rec_w, steps=ss)
        xs = _out_ffn(xs, attn, rec, wo_bf, g_ffn, wg_bf, wu_bf, wd_bf, gl, tm=bs * ss, final=final)
        ks_l.append(k.reshape(bs, ss, N_HEADS, HEAD_DIM))
        vs_l.append(v.reshape(bs, ss, N_HEADS, HEAD_DIM))
        hs_l.append(h_all.reshape(bs, ss, D_REC)[:, ss - 1])
        cs_l.append(xr.reshape(bs, ss, D_REC)[:, ss - (CONV_W - 1):])

    def token_major(t):
        return jnp.transpose(t.reshape(depth, bp, N_HEADS, HEAD_DIM, sp), (0, 1, 4, 2, 3))

    return (xp.reshape(bp, sp, D_MODEL), xs.reshape(bs, ss, D_MODEL),
            token_major(kv_t[0]), token_major(kv_t[1]), jnp.stack(hp_l), jnp.stack(cp_l),
            jnp.stack(ks_l), jnp.stack(vs_l), jnp.stack(hs_l), jnp.stack(cs_l))
```

```python
import functools

import jax
import jax.numpy as jnp
from jax import lax
from jax.experimental import pallas as pl
from jax.experimental.pallas import tpu as pltpu

D_MODEL = 1024
N_HEADS = 8
HEAD_DIM = 64
D_ATTN = N_HEADS * HEAD_DIM
N_REC_HEADS = 8
REC_BLOCK = 64
D_REC = N_REC_HEADS * REC_BLOCK
D_IN = 3 * D_ATTN + 2 * D_REC
CONV_W = 4
LRU_C = 8.0
MOBA_BLOCK = 256
MOBA_TOPK = 3
D_FF = 2816
ROPE_THETA = 10000.0
EPS = 1e-6
NEG = -1e30
LOG2E = 1.4426950408889634

LANES = 128
SUBLANES = 8
HEADS_PER_GROUP = LANES // HEAD_DIM
N_GROUPS = N_HEADS // HEADS_PER_GROUP
VMEM_LIMIT = 56 * 1024 * 1024

F32 = jnp.float32
BF16 = jnp.bfloat16
_NT = (((1,), (1,)), ((), ()))


def _params(semantics):
    return pltpu.CompilerParams(dimension_semantics=semantics, vmem_limit_bytes=VMEM_LIMIT)


def _resident(shape):
    zeros = (0,) * len(shape)
    return pl.BlockSpec(shape, lambda *_: zeros, pipeline_mode=pl.Buffered(1))


def _rmsnorm(x, g):
    y = x * lax.rsqrt(jnp.mean(x * x, axis=-1, keepdims=True) + EPS)
    return y * g


def _rope_angles(pos):
    half = HEAD_DIM // 2
    inv = ROPE_THETA ** (-jnp.arange(half, dtype=F32) / half)
    ang = pos.astype(F32)[:, None] * inv[None, :]
    return jnp.cos(ang), jnp.sin(ang)


def _rope_table(pos):
    cos, sin = _rope_angles(pos)
    zero = jnp.zeros_like(sin)
    reps = LANES // HEAD_DIM
    cos_t = jnp.tile(cos, (1, 2 * reps))
    sin_a = jnp.tile(jnp.concatenate([-sin, zero], axis=1), (1, reps))
    sin_b = jnp.tile(jnp.concatenate([zero, sin], axis=1), (1, reps))
    return jnp.concatenate([cos_t, sin_a, sin_b], axis=1)


def _rope_table_t(pos):
    cos, sin = _rope_angles(pos)
    return jnp.concatenate([cos, cos, -sin, sin], axis=1).T


def _rope_lanes(t, tab):
    cos, sin_a, sin_b = tab[:, :LANES], tab[:, LANES:2 * LANES], tab[:, 2 * LANES:]
    half = HEAD_DIM // 2
    pieces = []
    for j in range(D_ATTN // LANES):
        tj = t[:, j * LANES:(j + 1) * LANES]
        pieces.append(tj * cos + pltpu.roll(tj, LANES - half, axis=1) * sin_a
                      + pltpu.roll(tj, half, axis=1) * sin_b)
    return jnp.concatenate(pieces, axis=1)


def _rope_rows(t, tab_t):
    cos, sin = tab_t[:HEAD_DIM], tab_t[HEAD_DIM:]
    half = HEAD_DIM // 2
    pieces = []
    for h in range(N_HEADS):
        th = t[h * HEAD_DIM:(h + 1) * HEAD_DIM]
        partner = jnp.concatenate([th[half:], th[:half]], axis=0)
        pieces.append(th * cos + partner * sin)
    return jnp.concatenate(pieces, axis=0)


def _proj_prompt_kernel(x_ref, g_ref, wq_ref, wkv_ref, wr_ref, tab_ref, tabt_ref, *refs):
    q_ref, kt_ref, vt_ref, xr_ref, gr_ref = refs[-5:]
    hn = _rmsnorm(x_ref[...], g_ref[...]).astype(BF16)
    q = jnp.dot(hn, wq_ref[...], preferred_element_type=F32)
    q_ref[...] = (_rope_lanes(q, tab_ref[...]) * (HEAD_DIM ** -0.5 * LOG2E)).astype(BF16)
    kv_t = lax.dot_general(wkv_ref[...], hn, _NT, preferred_element_type=F32)
    kt_ref[0, 0] = _rope_rows(kv_t[:D_ATTN], tabt_ref[...])
    vt_ref[0, 0] = kv_t[D_ATTN:]
    r = jnp.dot(hn, wr_ref[...], preferred_element_type=F32)
    xr_ref[...] = r[:, :D_REC]
    gr_ref[...] = r[:, D_REC:]


def _proj_prompt(x, g, wq, wkv_t, wr, tab, tab_t, kv_prev, *, bsz, seq, tm, layer, depth):
    n = bsz * seq
    tps = seq // tm
    row = lambda b, j: (b * tps + j, 0)
    col = lambda b, j: (layer, b, 0, j)
    in_specs = [pl.BlockSpec((tm, D_MODEL), row),
                _resident((1, D_MODEL)),
                _resident((D_MODEL, D_ATTN)),
                _resident((2 * D_ATTN, D_MODEL)),
                _resident((D_MODEL, 2 * D_REC)),
                pl.BlockSpec((tm, 3 * LANES), lambda b, j: (j, 0)),
                pl.BlockSpec((2 * HEAD_DIM, tm), lambda b, j: (0, j))]
    args = [x, g, wq, wkv_t, wr, tab, tab_t]
    aliases = {len(args): 1, len(args) + 1: 2}
    in_specs += [pl.BlockSpec(memory_space=pl.ANY)] * 2
    args += list(kv_prev)
    return pl.pallas_call(
        _proj_prompt_kernel,
        grid=(bsz, tps),
        in_specs=in_specs,
        out_specs=[pl.BlockSpec((tm, D_ATTN), row),
                   pl.BlockSpec((1, 1, D_ATTN, tm), col), pl.BlockSpec((1, 1, D_ATTN, tm), col),
                   pl.BlockSpec((tm, D_REC), row), pl.BlockSpec((tm, D_REC), row)],
        out_shape=[jax.ShapeDtypeStruct((n, D_ATTN), BF16),
                   jax.ShapeDtypeStruct((depth, bsz, D_ATTN, seq), F32),
                   jax.ShapeDtypeStruct((depth, bsz, D_ATTN, seq), F32),
                   jax.ShapeDtypeStruct((n, D_REC), F32), jax.ShapeDtypeStruct((n, D_REC), F32)],
        input_output_aliases=aliases,
        compiler_params=_params(("parallel", "parallel")),
        name="proj_prompt",
    )(*args)


def _proj_sample_kernel(x_ref, g_ref, w_ref, tab_ref, q_ref, k_ref, v_ref, xr_ref, gr_ref):
    hn = _rmsnorm(x_ref[...], g_ref[...]).astype(BF16)
    proj = jnp.dot(hn, w_ref[...], preferred_element_type=F32)
    tab = tab_ref[...]
    q_ref[...] = (_rope_lanes(proj[:, :D_ATTN], tab) * (HEAD_DIM ** -0.5)).astype(BF16)
    k_ref[...] = _rope_lanes(proj[:, D_ATTN:2 * D_ATTN], tab)
    v_ref[...] = proj[:, 2 * D_ATTN:3 * D_ATTN]
    xr_ref[...] = proj[:, 3 * D_ATTN:3 * D_ATTN + D_REC]
    gr_ref[...] = proj[:, 3 * D_ATTN + D_REC:]


def _proj_sample(x, g, w_bf, tab):
    n = x.shape[0]
    full = lambda shape: pl.BlockSpec(shape, lambda i: (0,) * len(shape))
    return pl.pallas_call(
        _proj_sample_kernel,
        grid=(1,),
        in_specs=[full((n, D_MODEL)), full((1, D_MODEL)), full((D_MODEL, D_IN)), full((n, 3 * LANES))],
        out_specs=[full((n, D_ATTN))] * 3 + [full((n, D_REC))] * 2,
        out_shape=[jax.ShapeDtypeStruct((n, D_ATTN), BF16),
                   jax.ShapeDtypeStruct((n, D_ATTN), F32), jax.ShapeDtypeStruct((n, D_ATTN), F32),
                   jax.ShapeDtypeStruct((n, D_REC), F32), jax.ShapeDtypeStruct((n, D_REC), F32)],
        compiler_params=_params(("arbitrary",)),
        name="proj_sample",
    )(x, g, w_bf, tab)


def _top3_bias(scores, idx_f, n_valid, axis):
    sg = jnp.where(idx_f < n_valid, scores, NEG)
    bias = jnp.full(sg.shape, NEG, F32)
    for _ in range(MOBA_TOPK):
        m = jnp.max(sg, axis=axis, keepdims=True)
        idx = jnp.min(jnp.where(sg == m, idx_f, float(LANES)), axis=axis, keepdims=True)
        pick = idx_f == idx
        bias = jnp.where(pick & (m > 0.5 * NEG), 0.0, bias)
        sg = jnp.where(pick, NEG, sg)
    return bias


def _lane_tiles(x):
    return [x[:, j * LANES:(j + 1) * LANES] for j in range(x.shape[1] // LANES)]


def _tree(op, xs):
    while len(xs) > 1:
        xs = [op(xs[i], xs[i + 1]) for i in range(0, len(xs) - 1, 2)] + ([xs[-1]] if len(xs) % 2 else [])
    return xs[0]


def _moba_prompt_kernel(q_ref, kt_ref, vt_ref, o_ref, kb_ref, vb_ref, km_ref,
                        s0_ref, s1_ref, mg0_ref, mg1_ref, m_ref, l_ref, acc_ref):
    tile = pl.program_id(2)
    blk = MOBA_BLOCK
    tq = q_ref.shape[0]
    n_groups, _, gk = kb_ref.shape
    gb = gk // blk
    nblk = n_groups * gb
    nb8 = -(-nblk // SUBLANES) * SUBLANES

    @pl.when(tile == 0)
    def _():
        lane_k = lax.broadcasted_iota(jnp.int32, (LANES, LANES), 1)
        sub = lax.broadcasted_iota(jnp.int32, (LANES, gk), 0)
        blk_of_lane = lax.broadcasted_iota(jnp.int32, (LANES, gk), 1) // blk
        km = jnp.zeros((LANES, LANES), F32)
        for g in range(n_groups):
            kg = kt_ref[0, 0, :, g * gk:(g + 1) * gk]
            kb_ref[g, :LANES, :] = kg.astype(BF16)
            kb_ref[g, LANES:, :] = jnp.where(sub == g * gb + blk_of_lane, 1.0, 0.0).astype(BF16)
            vb_ref[g] = vt_ref[0, 0, :, g * gk:(g + 1) * gk].astype(BF16)
            for j in range(gb):
                mean = jnp.mean(kg[:, j * blk:(j + 1) * blk], axis=1, keepdims=True)
                km = jnp.where(lane_k == g * gb + j, mean, km)
        km_ref[...] = km.T

    q2 = q_ref[...]
    lane = lax.broadcasted_iota(jnp.int32, (tq, LANES), 1)
    cand = lax.broadcasted_iota(jnp.int32, (nb8, tq), 0)
    own = (tile * tq + lax.broadcasted_iota(jnp.int32, (nb8, tq), 1)) // blk
    cand_f, own_f = cand.astype(F32), own.astype(F32)
    g_last = (tile * tq) // gk
    key_pos = g_last * gk + lax.broadcasted_iota(jnp.int32, (tq, gk), 1)
    q_pos = tile * tq + lax.broadcasted_iota(jnp.int32, (tq, gk), 0)
    causal = key_pos <= q_pos

    heads = range(HEADS_PER_GROUP)
    in_head = [(lane >= h * HEAD_DIM) & (lane < (h + 1) * HEAD_DIM) for h in heads]
    q_aug = []
    for h in heads:
        qh = jnp.where(in_head[h], q2, jnp.zeros_like(q2))
        gate_t = lax.dot_general(km_ref[:nb8, :], qh.astype(F32), _NT, precision=lax.Precision.HIGHEST,
                                 preferred_element_type=F32)
        bias_t = _top3_bias(gate_t, cand_f, own_f, axis=0)
        bias_t = jnp.where(cand == own, 0.0, bias_t)
        bias = jnp.concatenate([bias_t, jnp.zeros((LANES - nb8, tq), F32)], axis=0).T
        q_aug.append(jnp.concatenate([qh, bias.astype(BF16)], axis=1))

    def logits(h, g):
        return jnp.dot(q_aug[h], kb_ref[g], preferred_element_type=F32)

    def stage_a(g, s_buf, mg_buf, masked):
        for h in heads:
            s = logits(h, g)
            if masked:
                s = jnp.where(causal, s, NEG)
            s_buf[h] = s
            mg_buf[h] = _tree(jnp.maximum, _lane_tiles(s))

    def stage_b(s_buf, mg_buf, g):
        vg = vb_ref[g]
        for h in heads:
            m_old = m_ref[h]
            m_new = jnp.maximum(m_old, jnp.broadcast_to(jnp.max(mg_buf[h], axis=1, keepdims=True),
                                                        (tq, LANES)))
            alpha = jnp.exp2(m_old - m_new)
            p = jnp.exp2(s_buf[h] - jnp.concatenate([m_new] * (gk // LANES), axis=1))
            m_ref[h] = m_new
            l_ref[h] = alpha * l_ref[h] + _tree(jnp.add, _lane_tiles(p))
            acc_ref[h] = alpha * acc_ref[h] + lax.dot_general(p.astype(BF16), vg, _NT,
                                                              preferred_element_type=F32)

    m_ref[...] = jnp.full(m_ref.shape, NEG, F32)
    l_ref[...] = jnp.zeros(l_ref.shape, F32)
    acc_ref[...] = jnp.zeros(acc_ref.shape, F32)

    stage_a(g_last, s0_ref, mg0_ref, True)
    n_pairs = g_last // 2

    def pair_body(j, _):
        stage_b(s0_ref, mg0_ref, jnp.where(j == 0, g_last, 2 * j - 1))
        stage_a(2 * j, s1_ref, mg1_ref, False)
        stage_b(s1_ref, mg1_ref, 2 * j)
        stage_a(2 * j + 1, s0_ref, mg0_ref, False)
        return 0

    lax.fori_loop(0, n_pairs, pair_body, 0)
    g_buffered = jnp.where(n_pairs == 0, g_last, 2 * n_pairs - 1)

    @pl.when(g_last % 2 == 1)
    def _():
        stage_b(s0_ref, mg0_ref, g_buffered)
        stage_a(g_last - 1, s1_ref, mg1_ref, False)
        stage_b(s1_ref, mg1_ref, g_last - 1)

    @pl.when(g_last % 2 == 0)
    def _():
        stage_b(s0_ref, mg0_ref, g_buffered)

    out = None
    for h in heads:
        oh = acc_ref[h] / jnp.sum(l_ref[h], axis=1, keepdims=True)
        out = oh if out is None else jnp.where(in_head[h], oh, out)
    o_ref[...] = out.astype(BF16)


GROUP_BLOCKS = 4
Q_TILE_BLOCKS = 2


def _moba_prompt(q_bf, k_t, v_t, *, bsz, seq, layer):
    nblk = seq // MOBA_BLOCK
    n = bsz * seq
    gb = min(GROUP_BLOCKS, nblk)
    gk = gb * MOBA_BLOCK
    tq = min(Q_TILE_BLOCKS, gb) * MOBA_BLOCK
    n_tiles = seq // tq
    assert nblk <= LANES and nblk % gb == 0 and gk % tq == 0
    hp = HEADS_PER_GROUP
    return pl.pallas_call(
        _moba_prompt_kernel,
        grid=(bsz, N_GROUPS, n_tiles),
        in_specs=[pl.BlockSpec((tq, LANES), lambda b, g, i: (b * n_tiles + i, g)),
                  pl.BlockSpec((1, 1, LANES, seq), lambda b, g, i: (layer, b, g, 0)),
                  pl.BlockSpec((1, 1, LANES, seq), lambda b, g, i: (layer, b, g, 0))],
        out_specs=pl.BlockSpec((tq, LANES), lambda b, g, i: (b * n_tiles + i, g)),
        out_shape=jax.ShapeDtypeStruct((n, D_ATTN), BF16),
        scratch_shapes=[pltpu.VMEM((nblk // gb, 2 * LANES, gk), BF16),
                        pltpu.VMEM((nblk // gb, LANES, gk), BF16),
                        pltpu.VMEM((LANES, LANES), F32)]
        + [pltpu.VMEM((hp, tq, gk), F32)] * 2
        + [pltpu.VMEM((hp, tq, LANES), F32)] * 5,
        compiler_params=_params(("arbitrary", "arbitrary", "arbitrary")),
        name="moba_prompt",
    )(q_bf, k_t, v_t)


def _moba_sample_kernel(pt_ref, q_ref, kn_ref, vn_ref, ck_hbm, cv_hbm, o_ref,
                        buf, sem, s_ref, p_ref, *, layer, n_pages, page, pages_per_chunk):
    b = pl.program_id(0)
    n_q = q_ref.shape[1]
    rows = n_q * N_HEADS
    chunk = pages_per_chunk * page
    n_chunks = n_pages // pages_per_chunk
    blocks_per_chunk = chunk // MOBA_BLOCK
    nblk = n_pages * page // MOBA_BLOCK

    n_slots = buf.shape[0]
    ahead = n_slots - 1
    n_loads = 2 * n_chunks
    assert n_loads % n_slots == 0 and ahead <= n_loads
    loads = [(ck_hbm, c) for c in range(n_chunks)] + [(cv_hbm, c) for c in range(n_chunks)]

    def chunk_copies(row, i):
        src, c = loads[i]
        return [pltpu.make_async_copy(src.at[layer, pt_ref[row, c * pages_per_chunk + j]],
                                      buf.at[i % n_slots, j], sem.at[i % n_slots])
                for j in range(pages_per_chunk)]

    def start(row, i):
        for cp in chunk_copies(row, i):
            cp.start()

    def start_ahead(i):
        nxt = i + ahead
        if nxt < n_loads:
            start(b, nxt)
        else:
            @pl.when(b + 1 < pl.num_programs(0))
            def _():
                start(b + 1, nxt - n_loads)

    def wait(i):
        for cp in chunk_copies(b, i):
            cp.wait()

    def chunk_value(i):
        return jnp.concatenate([buf[i % n_slots, j] for j in range(pages_per_chunk)], axis=1).astype(BF16)

    @pl.when(b == 0)
    def _():
        for i in range(ahead):
            start(b, i)

    q4 = q_ref[0]
    row_i = lax.broadcasted_iota(jnp.int32, (rows, D_ATTN), 0)
    col_i = lax.broadcasted_iota(jnp.int32, (rows, D_ATTN), 1)
    own_head = (col_i // HEAD_DIM) == (row_i % N_HEADS)
    q_rep = jnp.concatenate([jnp.broadcast_to(q4[t:t + 1], (N_HEADS, D_ATTN)) for t in range(n_q)], axis=0)
    q_bd = jnp.where(own_head, q_rep, jnp.zeros_like(q_rep))

    lane = lax.broadcasted_iota(jnp.int32, (rows, LANES), 1)
    gate = jnp.zeros((rows, LANES), F32)
    for i in range(n_chunks):
        wait(i)
        start_ahead(i)
        kc = chunk_value(i)
        s_c = jnp.dot(q_bd, kc, preferred_element_type=F32)
        s_ref[:, i * chunk:(i + 1) * chunk] = s_c
        for j in range(blocks_per_chunk):
            tot = jnp.sum(s_c[:, j * MOBA_BLOCK:(j + 1) * MOBA_BLOCK], axis=1, keepdims=True)
            gate = jnp.where(lane == i * blocks_per_chunk + j, tot * (1.0 / MOBA_BLOCK), gate)

    bias = _top3_bias(gate, lane.astype(F32), float(nblk), axis=1)

    tok = lax.broadcasted_iota(jnp.int32, (rows, 1), 0) // N_HEADS
    q_bd_f = q_bd.astype(F32)
    kn = kn_ref[0]
    vn = vn_ref[0]
    s_new = []
    for j in range(n_q):
        sj = jnp.sum(q_bd_f * kn[j:j + 1, :], axis=1, keepdims=True)
        s_new.append(jnp.where(tok >= j, sj, NEG))

    m = s_new[0]
    for sj in s_new[1:]:
        m = jnp.maximum(m, sj)
    bias_cols = []
    for n in range(nblk):
        bcol = jnp.max(jnp.where(lane == n, bias, NEG), axis=1, keepdims=True)
        bias_cols.append(bcol)
        sb = s_ref[:, n * MOBA_BLOCK:(n + 1) * MOBA_BLOCK] + bcol
        m = jnp.maximum(m, jnp.max(sb, axis=1, keepdims=True))
    p_new = [jnp.exp(sj - m) for sj in s_new]
    l = p_new[0]
    for pj in p_new[1:]:
        l = l + pj
    for n in range(nblk):
        pb = jnp.exp(s_ref[:, n * MOBA_BLOCK:(n + 1) * MOBA_BLOCK] + bias_cols[n] - m)
        l = l + jnp.sum(pb, axis=1, keepdims=True)
        p_ref[:, n * MOBA_BLOCK:(n + 1) * MOBA_BLOCK] = pb.astype(BF16)

    acc = jnp.zeros((rows, D_ATTN), F32)
    for i in range(n_chunks, 2 * n_chunks):
        wait(i)
        start_ahead(i)
        c = i - n_chunks
        vc = chunk_value(i)
        acc = acc + lax.dot_general(p_ref[:, c * chunk:(c + 1) * chunk], vc, _NT,
                                    preferred_element_type=F32)
    for j in range(n_q):
        acc = acc + p_new[j] * vn[j:j + 1, :]
    out = jnp.where(own_head, acc / l, 0.0)
    out = jnp.sum(out.reshape(n_q, N_HEADS, D_ATTN), axis=1)
    o_ref[0] = out.astype(BF16)


SAMPLE_SLOTS = 4


def _moba_sample(page_table, q_bf, k_new, v_new, ck_t, cv_t, *, layer):
    bsz, n_pages = page_table.shape
    page = ck_t.shape[3]
    n_q = q_bf.shape[0] // bsz
    rows = n_q * N_HEADS
    pages_per_chunk = min(8, n_pages)
    chunk = pages_per_chunk * page
    n_slots = min(SAMPLE_SLOTS, 2 * n_pages // pages_per_chunk)
    past = n_pages * page
    assert n_pages % pages_per_chunk == 0 and chunk % MOBA_BLOCK == 0 and past // MOBA_BLOCK <= LANES
    blk3 = lambda b, pt: (b, 0, 0)
    kernel = functools.partial(_moba_sample_kernel, layer=layer, n_pages=n_pages, page=page,
                               pages_per_chunk=pages_per_chunk)
    out = pl.pallas_call(
        kernel,
        grid_spec=pltpu.PrefetchScalarGridSpec(
            num_scalar_prefetch=1,
            grid=(bsz,),
            in_specs=[pl.BlockSpec((1, n_q, D_ATTN), blk3),
                      pl.BlockSpec((1, n_q, D_ATTN), blk3),
                      pl.BlockSpec((1, n_q, D_ATTN), blk3),
                      pl.BlockSpec(memory_space=pl.ANY),
                      pl.BlockSpec(memory_space=pl.ANY)],
            out_specs=pl.BlockSpec((1, n_q, D_ATTN), blk3),
            scratch_shapes=[pltpu.VMEM((n_slots, pages_per_chunk, D_ATTN, page), F32),
                            pltpu.SemaphoreType.DMA((n_slots,)),
                            pltpu.VMEM((rows, past), F32),
                            pltpu.VMEM((rows, past), BF16)]),
        out_shape=jax.ShapeDtypeStruct((bsz, n_q, D_ATTN), BF16),
        compiler_params=_params(("arbitrary",)),
        name="moba_sample",
    )(page_table, q_bf.reshape(bsz, n_q, D_ATTN), k_new.reshape(bsz, n_q, D_ATTN),
      v_new.reshape(bsz, n_q, D_ATTN), ck_t, cv_t)
    return out.reshape(bsz * n_q, D_ATTN)


def _lru_terms(xc, wa_ref, ba_ref, wx_ref, bx_ref, lam_ref):
    xcb = xc.astype(BF16)
    r = jax.nn.sigmoid(jnp.dot(xcb, wa_ref[...], preferred_element_type=F32) + ba_ref[...])
    gi = jax.nn.sigmoid(jnp.dot(xcb, wx_ref[...], preferred_element_type=F32) + bx_ref[...])
    nlam = -lam_ref[...]
    softplus = jnp.maximum(nlam, 0.0) + jnp.log1p(jnp.exp(-jnp.abs(nlam)))
    log_a = -LRU_C * r * softplus
    a = jnp.exp(log_a)
    bt = jnp.sqrt(1.0 - a * a) * (gi * xc)
    return a, bt


def _rec_prompt_kernel(xr_ref, gr_ref, cw_ref, cb_ref, wa_ref, ba_ref, wx_ref, bx_ref, lam_ref,
                       rec_ref, hlast_ref, tail_ref, h_ref):
    @pl.when(pl.program_id(1) == 0)
    def _():
        tail_ref[...] = jnp.zeros_like(tail_ref)
        h_ref[...] = jnp.zeros_like(h_ref)

    xr = xr_ref[...]
    t_rows = xr.shape[0]
    xp = jnp.concatenate([tail_ref[...], xr], axis=0)
    tail_ref[...] = xr[t_rows - SUBLANES:, :]
    cw = cw_ref[...]
    xc = cb_ref[...]
    for j in range(CONV_W):
        off = SUBLANES - (CONV_W - 1) + j
        xc = xc + xp[off:off + t_rows, :] * cw[j:j + 1, :]

    a, bt = _lru_terms(xc, wa_ref, ba_ref, wx_ref, bx_ref, lam_ref)

    sub = lax.broadcasted_iota(jnp.int32, a.shape, 0) % SUBLANES
    d = 1
    while d < SUBLANES:
        keep = sub >= d
        a_sh = pltpu.roll(a, d, axis=0)
        b_sh = pltpu.roll(bt, d, axis=0)
        bt = jnp.where(keep, a * b_sh + bt, bt)
        a = jnp.where(keep, a * a_sh, a)
        d *= 2
    h = h_ref[0:1, :]
    outs = []
    for g in range(t_rows // SUBLANES):
        hg = a[g * SUBLANES:(g + 1) * SUBLANES] * h + bt[g * SUBLANES:(g + 1) * SUBLANES]
        outs.append(hg)
        h = hg[SUBLANES - 1:SUBLANES, :]
    hs = jnp.concatenate(outs, axis=0)
    h_ref[...] = jnp.broadcast_to(h, h_ref.shape)
    hlast_ref[0] = h
    rec_ref[...] = (hs * jax.nn.gelu(gr_ref[...])).astype(BF16)


def _rec_prompt(xr, gr, cw, cb, wa_bd, ba, wx_bd, bx, lam, *, bsz, seq, tile):
    n = bsz * seq
    tps = seq // tile
    row = lambda b, j: (b * tps + j, 0)
    return pl.pallas_call(
        _rec_prompt_kernel,
        grid=(bsz, tps),
        in_specs=[pl.BlockSpec((tile, D_REC), row), pl.BlockSpec((tile, D_REC), row),
                  _resident((CONV_W, D_REC)), _resident((1, D_REC)),
                  _resident((D_REC, D_REC)), _resident((1, D_REC)),
                  _resident((D_REC, D_REC)), _resident((1, D_REC)), _resident((1, D_REC))],
        out_specs=[pl.BlockSpec((tile, D_REC), row),
                   pl.BlockSpec((1, 1, D_REC), lambda b, j: (b, 0, 0))],
        out_shape=[jax.ShapeDtypeStruct((n, D_REC), BF16),
                   jax.ShapeDtypeStruct((bsz, 1, D_REC), F32)],
        scratch_shapes=[pltpu.VMEM((SUBLANES, D_REC), F32), pltpu.VMEM((SUBLANES, D_REC), F32)],
        compiler_params=_params(("arbitrary", "arbitrary")),
        name="rec_prompt",
    )(xr, gr, cw, cb, wa_bd, ba, wx_bd, bx, lam)


def _rec_sample_kernel(xr_ref, gr_ref, prev_ref, h0_ref, cw_ref, cb_ref, wa_ref, ba_ref, wx_ref, bx_ref,
                       lam_ref, rec_ref, h_ref, *, steps):
    x = xr_ref[...]
    prev = prev_ref[...]
    n = x.shape[0]
    t = lax.broadcasted_iota(jnp.int32, x.shape, 0) % steps
    cw = cw_ref[...]
    xc = cb_ref[...] + x * cw[CONV_W - 1:CONV_W, :]
    for d in range(1, CONV_W):
        from_x = pltpu.roll(x, d, axis=0)
        from_prev = pltpu.roll(prev, (n - (CONV_W - 1 - d)) % n, axis=0)
        xc = xc + jnp.where(t >= d, from_x, from_prev) * cw[CONV_W - 1 - d:CONV_W - d, :]

    a, bt = _lru_terms(xc, wa_ref, ba_ref, wx_ref, bx_ref, lam_ref)
    d = 1
    while d < steps:
        keep = t >= d
        a_sh = pltpu.roll(a, d, axis=0)
        b_sh = pltpu.roll(bt, d, axis=0)
        bt = jnp.where(keep, a * b_sh + bt, bt)
        a = jnp.where(keep, a * a_sh, a)
        d *= 2
    h = a * h0_ref[...] + bt
    h_ref[...] = h
    rec_ref[...] = (h * jax.nn.gelu(gr_ref[...])).astype(BF16)


def _rec_sample(xr, gr, prev, h0_rep, cw, cb, wa_bd, ba, wx_bd, bx, lam, *, steps):
    n = xr.shape[0]
    full = lambda shape: pl.BlockSpec(shape, lambda i: (0,) * len(shape))
    return pl.pallas_call(
        functools.partial(_rec_sample_kernel, steps=steps),
        grid=(1,),
        in_specs=[full((n, D_REC))] * 4 + [full((CONV_W, D_REC)), full((1, D_REC)),
                                            full((D_REC, D_REC)), full((1, D_REC)),
                                            full((D_REC, D_REC)), full((1, D_REC)), full((1, D_REC))],
        out_specs=[full((n, D_REC)), full((n, D_REC))],
        out_shape=[jax.ShapeDtypeStruct((n, D_REC), BF16), jax.ShapeDtypeStruct((n, D_REC), F32)],
        compiler_params=_params(("arbitrary",)),
        name="rec_sample",
    )(xr, gr, prev, h0_rep, cw, cb, wa_bd, ba, wx_bd, bx, lam)


MXU_DIM = 256
_FF_SPLIT = (D_FF // MXU_DIM + 1) // 2 * MXU_DIM
FF_CHUNKS = ((0, _FF_SPLIT), (_FF_SPLIT, D_FF))


def _out_ffn_kernel(x_ref, attn_ref, rec_ref, wo_ref, gf_ref, wg_ref, wu_ref, wd_ref, gl_ref, o_ref, *, final):
    mixed = jnp.concatenate([attn_ref[...], rec_ref[...]], axis=1)
    x1 = x_ref[...] + jnp.dot(mixed, wo_ref[...], preferred_element_type=F32)
    hn = _rmsnorm(x1, gf_ref[...]).astype(BF16)
    y = x1
    for lo, hi in FF_CHUNKS:
        g = jnp.dot(hn, wg_ref[:, lo:hi], preferred_element_type=F32)
        u = jnp.dot(hn, wu_ref[:, lo:hi], preferred_element_type=F32)
        act = (jax.nn.silu(g) * u).astype(BF16)
        y = y + jnp.dot(act, wd_ref[lo:hi, :], preferred_element_type=F32)
    if final:
        y = _rmsnorm(y, gl_ref[...])
    o_ref[...] = y


def _out_ffn(x, attn, rec, wo, gf, wg, wu, wd, gl, *, tm, final):
    n = x.shape[0]
    row = lambda i: (i, 0)
    return pl.pallas_call(
        functools.partial(_out_ffn_kernel, final=final),
        grid=(n // tm,),
        in_specs=[pl.BlockSpec((tm, D_MODEL), row), pl.BlockSpec((tm, D_ATTN), row),
                  pl.BlockSpec((tm, D_REC), row),
                  _resident((D_ATTN + D_REC, D_MODEL)), _resident((1, D_MODEL)),
                  _resident((D_MODEL, D_FF)), _resident((D_MODEL, D_FF)), _resident((D_FF, D_MODEL)),
                  _resident((1, D_MODEL))],
        out_specs=pl.BlockSpec((tm, D_MODEL), row),
        out_shape=jax.ShapeDtypeStruct((n, D_MODEL), F32),
        compiler_params=_params(("parallel",)),
        name="out_ffn",
    )(x, attn, rec, wo, gf, wg, wu, wd, gl)


def _block_diag(w):
    h, n, _ = w.shape
    eye = jnp.eye(h, dtype=w.dtype)
    return (eye[:, None, :, None] * w[:, :, None, :]).reshape(h * n, h * n)


def kernel(x_prompt, x_sample, cache_k, cache_v, state_h, state_conv, page_table, norm_mix, w_in, conv_w, conv_b, w_gate_a, b_gate_a, w_gate_x, b_gate_x, lru_lambda, w_out, norm_ffn, w_ff_gate, w_ff_up, w_ff_down, norm_final):
    bp, sp, _ = x_prompt.shape
    bs, ss, _ = x_sample.shape
    depth = w_in.shape[0]
    n_phys, page = cache_k.shape[1:3]
    past = page_table.shape[1] * page
    assert sp % MOBA_BLOCK == 0 and past % MOBA_BLOCK == 0 and ss >= CONV_W - 1

    tm_p = 512
    pos_p = jnp.arange(sp, dtype=jnp.int32)
    tab_p, tab_pt = _rope_table(pos_p), _rope_table_t(pos_p)
    tab_s = _rope_table(past + (jnp.arange(bs * ss, dtype=jnp.int32) % ss))

    ck_t = jnp.transpose(cache_k, (0, 1, 3, 4, 2)).reshape(depth, n_phys, D_ATTN, page)
    cv_t = jnp.transpose(cache_v, (0, 1, 3, 4, 2)).reshape(depth, n_phys, D_ATTN, page)

    row2 = lambda a: a.reshape(1, -1)
    xp = x_prompt.reshape(bp * sp, D_MODEL)
    xs = x_sample.reshape(bs * ss, D_MODEL)
    gl = row2(norm_final)
    hp_l, cp_l, ks_l, vs_l, hs_l, cs_l = [], [], [], [], [], []
    kv_t = (jnp.zeros((depth, bp, D_ATTN, sp), F32), jnp.zeros((depth, bp, D_ATTN, sp), F32))
    for l in range(depth):
        final = l == depth - 1
        w_in_bf = w_in[l].astype(BF16)
        wq = w_in_bf[:, :D_ATTN]
        wkv_t = w_in_bf[:, D_ATTN:3 * D_ATTN].T
        wr = w_in_bf[:, 3 * D_ATTN:]
        wo_bf = w_out[l].astype(BF16)
        wg_bf, wu_bf, wd_bf = w_ff_gate[l].astype(BF16), w_ff_up[l].astype(BF16), w_ff_down[l].astype(BF16)
        wa_bd = _block_diag(w_gate_a[l]).astype(BF16)
        wx_bd = _block_diag(w_gate_x[l]).astype(BF16)
        g_mix, g_ffn = row2(norm_mix[l]), row2(norm_ffn[l])
        rec_w = (conv_w[l], row2(conv_b[l]), wa_bd, row2(b_gate_a[l]), wx_bd, row2(b_gate_x[l]),
                 row2(lru_lambda[l]))

        q_bf, k_t, v_t, xr, gr = _proj_prompt(xp, g_mix, wq, wkv_t, wr, tab_p, tab_pt, kv_t, bsz=bp, seq=sp,
                                              tm=tm_p, layer=l, depth=depth)
        kv_t = (k_t, v_t)
        attn = _moba_prompt(q_bf, k_t, v_t, bsz=bp, seq=sp, layer=l)
        rec, h_last = _rec_prompt(xr, gr, *rec_w, bsz=bp, seq=sp, tile=tm_p)
        xp = _out_ffn(xp, attn, rec, wo_bf, g_ffn, wg_bf, wu_bf, wd_bf, gl, tm=tm_p, final=final)
        hp_l.append(h_last.reshape(bp, D_REC))
        cp_l.append(xr.reshape(bp, sp, D_REC)[:, sp - (CONV_W - 1):])

        q_bf, k, v, xr, gr = _proj_sample(xs, g_mix, w_in_bf, tab_s)
        attn = _moba_sample(page_table, q_bf, k, v, ck_t, cv_t, layer=l)
        prev = jnp.pad(state_conv[l], ((0, 0), (0, ss - (CONV_W - 1)), (0, 0))).reshape(bs * ss, D_REC)
        h0_rep = jnp.repeat(state_h[l], ss, axis=0)
        rec, h_all = _rec_sample(xr, gr, prev, h0_rep, *rec_w, steps=ss)
        xs = _out_ffn(xs, attn, rec, wo_bf, g_ffn, wg_bf, wu_bf, wd_bf, gl, tm=bs * ss, final=final)
        ks_l.append(k.reshape(bs, ss, N_HEADS, HEAD_DIM))
        vs_l.append(v.reshape(bs, ss, N_HEADS, HEAD_DIM))
        hs_l.append(h_all.reshape(bs, ss, D_REC)[:, ss - 1])
        cs_l.append(xr.reshape(bs, ss, D_REC)[:, ss - (CONV_W - 1):])

    def token_major(t):
        return jnp.transpose(t.reshape(depth, bp, N_HEADS, HEAD_DIM, sp), (0, 1, 4, 2, 3))

    return (xp.reshape(bp, sp, D_MODEL), xs.reshape(bs, ss, D_MODEL),
            token_major(kv_t[0]), token_major(kv_t[1]), jnp.stack(hp_l), jnp.stack(cp_l),
            jnp.stack(ks_l), jnp.stack(vs_l), jnp.stack(hs_l), jnp.stack(cs_l))
```

```python
import functools

import jax
import jax.numpy as jnp
from jax import lax
from jax.experimental import pallas as pl
from jax.experimental.pallas import tpu as pltpu

D_MODEL = 1024
N_HEADS = 8
HEAD_DIM = 64
D_ATTN = N_HEADS * HEAD_DIM
N_REC_HEADS = 8
REC_BLOCK = 64
D_REC = N_REC_HEADS * REC_BLOCK
D_IN = 3 * D_ATTN + 2 * D_REC
CONV_W = 4
LRU_C = 8.0
MOBA_BLOCK = 256
MOBA_TOPK = 3
D_FF = 2816
ROPE_THETA = 10000.0
EPS = 1e-6
NEG = -1e30
LOG2E = 1.4426950408889634

LANES = 128
SUBLANES = 8
HEADS_PER_GROUP = LANES // HEAD_DIM
N_GROUPS = N_HEADS // HEADS_PER_GROUP
VMEM_LIMIT = 56 * 1024 * 1024

F32 = jnp.float32
BF16 = jnp.bfloat16
_NT = (((1,), (1,)), ((), ()))


def _params(semantics):
    return pltpu.CompilerParams(dimension_semantics=semantics, vmem_limit_bytes=VMEM_LIMIT)


def _resident(shape):
    zeros = (0,) * len(shape)
    return pl.BlockSpec(shape, lambda *_: zeros, pipeline_mode=pl.Buffered(1))


def _rmsnorm(x, g):
    y = x * lax.rsqrt(jnp.mean(x * x, axis=-1, keepdims=True) + EPS)
    return y * g


def _rope_angles(pos):
    half = HEAD_DIM // 2
    inv = ROPE_THETA ** (-jnp.arange(half, dtype=F32) / half)
    ang = pos.astype(F32)[:, None] * inv[None, :]
    return jnp.cos(ang), jnp.sin(ang)


def _rope_table(pos):
    cos, sin = _rope_angles(pos)
    zero = jnp.zeros_like(sin)
    reps = LANES // HEAD_DIM
    cos_t = jnp.tile(cos, (1, 2 * reps))
    sin_a = jnp.tile(jnp.concatenate([-sin, zero], axis=1), (1, reps))
    sin_b = jnp.tile(jnp.concatenate([zero, sin], axis=1), (1, reps))
    return jnp.concatenate([cos_t, sin_a, sin_b], axis=1)


def _rope_table_t(pos):
    cos, sin = _rope_angles(pos)
    return jnp.concatenate([cos, cos, -sin, sin], axis=1).T


def _rope_lanes(t, tab):
    cos, sin_a, sin_b = tab[:, :LANES], tab[:, LANES:2 * LANES], tab[:, 2 * LANES:]
    half = HEAD_DIM // 2
    pieces = []
    for j in range(D_ATTN // LANES):
        tj = t[:, j * LANES:(j + 1) * LANES]
        pieces.append(tj * cos + pltpu.roll(tj, LANES - half, axis=1) * sin_a
                      + pltpu.roll(tj, half, axis=1) * sin_b)
    return jnp.concatenate(pieces, axis=1)


def _rope_rows(t, tab_t):
    cos, sin = tab_t[:HEAD_DIM], tab_t[HEAD_DIM:]
    half = HEAD_DIM // 2
    pieces = []
    for h in range(N_HEADS):
        th = t[h * HEAD_DIM:(h + 1) * HEAD_DIM]
        partner = jnp.concatenate([th[half:], th[:half]], axis=0)
        pieces.append(th * cos + partner * sin)
    return jnp.concatenate(pieces, axis=0)


def _sigmoid(z):
    return 0.5 * jnp.tanh(0.5 * z) + 0.5


def _lru_terms(xc, wa_ref, ba_ref, wx_ref, bx_ref, lam_ref):
    xcb = xc.astype(BF16)
    r = _sigmoid(jnp.dot(xcb, wa_ref[...], preferred_element_type=F32) + ba_ref[...])
    gi = _sigmoid(jnp.dot(xcb, wx_ref[...], preferred_element_type=F32) + bx_ref[...])
    nlam = -lam_ref[...]
    softplus = jnp.maximum(nlam, 0.0) + jnp.log1p(jnp.exp(-jnp.abs(nlam)))
    log_a = -LRU_C * r * softplus
    a = jnp.exp(log_a)
    bt = jnp.sqrt(1.0 - a * a) * (gi * xc)
    return a, bt


def _rec_tile(xr, gr, tail_ref, h_ref, cw_ref, cb_ref, wa_ref, ba_ref, wx_ref, bx_ref, lam_ref):
    t_rows = xr.shape[0]
    tail = tail_ref[...]
    tail_ref[...] = xr[t_rows - SUBLANES:, :]
    cw = cw_ref[...]
    sub8 = lax.broadcasted_iota(jnp.int32, tail.shape, 0)
    xc = cb_ref[...] + xr * cw[CONV_W - 1:CONV_W, :]
    for d in range(1, CONV_W):
        rolled = pltpu.roll(xr, d, axis=0)
        first = jnp.where(sub8 < d, pltpu.roll(tail, d, axis=0), rolled[:SUBLANES])
        shifted = jnp.concatenate([first, rolled[SUBLANES:]], axis=0)
        xc = xc + shifted * cw[CONV_W - 1 - d:CONV_W - d, :]

    a, bt = _lru_terms(xc, wa_ref, ba_ref, wx_ref, bx_ref, lam_ref)

    n_groups = t_rows // SUBLANES
    a = a.reshape(n_groups, SUBLANES, D_REC)
    bt = bt.reshape(n_groups, SUBLANES, D_REC)
    sub = lax.broadcasted_iota(jnp.int32, a.shape, 1)
    d = 1
    while d < SUBLANES:
        keep = sub >= d
        a_sh = pltpu.roll(a, d, axis=1)
        b_sh = pltpu.roll(bt, d, axis=1)
        bt = jnp.where(keep, a * b_sh + bt, bt)
        a = jnp.where(keep, a * a_sh, a)
        d *= 2
    h = h_ref[0:1, :]
    outs = []
    for g in range(n_groups):
        hg = a[g] * h + bt[g]
        outs.append(hg)
        h = hg[SUBLANES - 1:SUBLANES, :]
    hs = jnp.concatenate(outs, axis=0)
    h_ref[...] = jnp.broadcast_to(h, h_ref.shape)
    return (hs * jax.nn.gelu(gr)).astype(BF16), h


def _proj_rec_prompt_kernel(x_ref, g_ref, wq_ref, wkv_ref, wr_ref, tab_ref, tabt_ref,
                            cw_ref, cb_ref, wa_ref, ba_ref, wx_ref, bx_ref, lam_ref, kt_all, vt_all,
                            q_ref, kt_ref, vt_ref, rec_ref, hlast_ref, ctail_ref, tail_ref, h_ref):
    del kt_all, vt_all

    @pl.when(pl.program_id(1) == 0)
    def _():
        tail_ref[...] = jnp.zeros_like(tail_ref)
        h_ref[...] = jnp.zeros_like(h_ref)

    hn = _rmsnorm(x_ref[...], g_ref[...]).astype(BF16)
    r = jnp.dot(hn, wr_ref[...], preferred_element_type=F32)
    xr = r[:, :D_REC]
    rec, h_last = _rec_tile(xr, r[:, D_REC:], tail_ref, h_ref,
                            cw_ref, cb_ref, wa_ref, ba_ref, wx_ref, bx_ref, lam_ref)
    rec_ref[...] = rec
    hlast_ref[0] = h_last
    ctail_ref[0] = xr[xr.shape[0] - SUBLANES:, :]

    q = jnp.dot(hn, wq_ref[...], preferred_element_type=F32)
    q_ref[...] = (_rope_lanes(q, tab_ref[...]) * (HEAD_DIM ** -0.5 * LOG2E)).astype(BF16)
    kv_t = lax.dot_general(wkv_ref[...], hn, _NT, preferred_element_type=F32)
    kt_ref[0, 0] = _rope_rows(kv_t[:D_ATTN], tabt_ref[...])
    vt_ref[0, 0] = kv_t[D_ATTN:]


def _proj_rec_prompt(x, g, wq, wkv_t, wr, tab, tab_t, rec_w, kv_prev, *, bsz, seq, tm, layer, depth):
    n = bsz * seq
    tps = seq // tm
    row = lambda b, j: (b * tps + j, 0)
    col = lambda b, j: (layer, b, 0, j)
    per_b = lambda b, j: (b, 0, 0)
    args = [x, g, wq, wkv_t, wr, tab, tab_t, *rec_w]
    in_specs = [pl.BlockSpec((tm, D_MODEL), row),
                _resident((1, D_MODEL)),
                _resident((D_MODEL, D_ATTN)),
                _resident((2 * D_ATTN, D_MODEL)),
                _resident((D_MODEL, 2 * D_REC)),
                pl.BlockSpec((tm, 3 * LANES), lambda b, j: (j, 0)),
                pl.BlockSpec((2 * HEAD_DIM, tm), lambda b, j: (0, j)),
                _resident((CONV_W, D_REC)), _resident((1, D_REC)),
                _resident((D_REC, D_REC)), _resident((1, D_REC)),
                _resident((D_REC, D_REC)), _resident((1, D_REC)), _resident((1, D_REC)),
                pl.BlockSpec(memory_space=pl.ANY), pl.BlockSpec(memory_space=pl.ANY)]
    return pl.pallas_call(
        _proj_rec_prompt_kernel,
        grid=(bsz, tps),
        in_specs=in_specs,
        out_specs=[pl.BlockSpec((tm, D_ATTN), row),
                   pl.BlockSpec((1, 1, D_ATTN, tm), col), pl.BlockSpec((1, 1, D_ATTN, tm), col),
                   pl.BlockSpec((tm, D_REC), row),
                   pl.BlockSpec((1, 1, D_REC), per_b), pl.BlockSpec((1, SUBLANES, D_REC), per_b)],
        out_shape=[jax.ShapeDtypeStruct((n, D_ATTN), BF16),
                   jax.ShapeDtypeStruct((depth, bsz, D_ATTN, seq), F32),
                   jax.ShapeDtypeStruct((depth, bsz, D_ATTN, seq), F32),
                   jax.ShapeDtypeStruct((n, D_REC), BF16),
                   jax.ShapeDtypeStruct((bsz, 1, D_REC), F32),
                   jax.ShapeDtypeStruct((bsz, SUBLANES, D_REC), F32)],
        scratch_shapes=[pltpu.VMEM((SUBLANES, D_REC), F32), pltpu.VMEM((SUBLANES, D_REC), F32)],
        input_output_aliases={len(args): 1, len(args) + 1: 2},
        compiler_params=_params(("arbitrary", "arbitrary")),
        name="proj_rec_prompt",
    )(*args, *kv_prev)


def _proj_sample_kernel(x_ref, g_ref, w_ref, tab_ref, q_ref, k_ref, v_ref, xr_ref, gr_ref):
    hn = _rmsnorm(x_ref[...], g_ref[...]).astype(BF16)
    proj = jnp.dot(hn, w_ref[...], preferred_element_type=F32)
    tab = tab_ref[...]
    q_ref[...] = (_rope_lanes(proj[:, :D_ATTN], tab) * (HEAD_DIM ** -0.5)).astype(BF16)
    k_ref[...] = _rope_lanes(proj[:, D_ATTN:2 * D_ATTN], tab)
    v_ref[...] = proj[:, 2 * D_ATTN:3 * D_ATTN]
    xr_ref[...] = proj[:, 3 * D_ATTN:3 * D_ATTN + D_REC]
    gr_ref[...] = proj[:, 3 * D_ATTN + D_REC:]


def _proj_sample(x, g, w_bf, tab):
    n = x.shape[0]
    full = lambda shape: pl.BlockSpec(shape, lambda i: (0,) * len(shape))
    return pl.pallas_call(
        _proj_sample_kernel,
        grid=(1,),
        in_specs=[full((n, D_MODEL)), full((1, D_MODEL)), full((D_MODEL, D_IN)), full((n, 3 * LANES))],
        out_specs=[full((n, D_ATTN))] * 3 + [full((n, D_REC))] * 2,
        out_shape=[jax.ShapeDtypeStruct((n, D_ATTN), BF16),
                   jax.ShapeDtypeStruct((n, D_ATTN), F32), jax.ShapeDtypeStruct((n, D_ATTN), F32),
                   jax.ShapeDtypeStruct((n, D_REC), F32), jax.ShapeDtypeStruct((n, D_REC), F32)],
        compiler_params=_params(("arbitrary",)),
        name="proj_sample",
    )(x, g, w_bf, tab)


def _top3_bias(scores, idx_f, n_valid, axis):
    sg = jnp.where(idx_f < n_valid, scores, NEG)
    bias = jnp.full(sg.shape, NEG, F32)
    for _ in range(MOBA_TOPK):
        m = jnp.max(sg, axis=axis, keepdims=True)
        idx = jnp.min(jnp.where(sg == m, idx_f, float(LANES)), axis=axis, keepdims=True)
        pick = idx_f == idx
        bias = jnp.where(pick & (m > 0.5 * NEG), 0.0, bias)
        sg = jnp.where(pick, NEG, sg)
    return bias


def _lane_tiles(x):
    return [x[:, j * LANES:(j + 1) * LANES] for j in range(x.shape[1] // LANES)]


def _tree(op, xs):
    while len(xs) > 1:
        xs = [op(xs[i], xs[i + 1]) for i in range(0, len(xs) - 1, 2)] + ([xs[-1]] if len(xs) % 2 else [])
    return xs[0]


def _moba_prompt_kernel(q_ref, kt_ref, vt_ref, o_ref, kb_ref, vb_ref, km_ref,
                        s0_ref, s1_ref, mg0_ref, mg1_ref, m_ref, l_ref, acc_ref):
    tile = pl.program_id(2)
    blk = MOBA_BLOCK
    tq = q_ref.shape[0]
    n_groups, _, gk = kb_ref.shape
    gb = gk // blk
    nblk = n_groups * gb
    nb8 = -(-nblk // SUBLANES) * SUBLANES

    @pl.when(tile == 0)
    def _():
        lane_k = lax.broadcasted_iota(jnp.int32, (LANES, LANES), 1)
        sub = lax.broadcasted_iota(jnp.int32, (LANES, gk), 0)
        blk_of_lane = lax.broadcasted_iota(jnp.int32, (LANES, gk), 1) // blk
        km = jnp.zeros((LANES, LANES), F32)
        for g in range(n_groups):
            kg = kt_ref[0, 0, :, g * gk:(g + 1) * gk]
            kb_ref[g, :LANES, :] = kg.astype(BF16)
            kb_ref[g, LANES:, :] = jnp.where(sub == g * gb + blk_of_lane, 1.0, 0.0).astype(BF16)
            vb_ref[g] = vt_ref[0, 0, :, g * gk:(g + 1) * gk].astype(BF16)
            for j in range(gb):
                mean = jnp.mean(kg[:, j * blk:(j + 1) * blk], axis=1, keepdims=True)
                km = jnp.where(lane_k == g * gb + j, mean, km)
        km_ref[...] = km.T

    q2 = q_ref[...]
    lane = lax.broadcasted_iota(jnp.int32, (tq, LANES), 1)
    cand = lax.broadcasted_iota(jnp.int32, (nb8, tq), 0)
    own = (tile * tq + lax.broadcasted_iota(jnp.int32, (nb8, tq), 1)) // blk
    cand_f, own_f = cand.astype(F32), own.astype(F32)
    g_last = (tile * tq) // gk
    key_pos = g_last * gk + lax.broadcasted_iota(jnp.int32, (tq, gk), 1)
    q_pos = tile * tq + lax.broadcasted_iota(jnp.int32, (tq, gk), 0)
    causal = key_pos <= q_pos

    heads = range(HEADS_PER_GROUP)
    in_head = [(lane >= h * HEAD_DIM) & (lane < (h + 1) * HEAD_DIM) for h in heads]
    q_aug = []
    for h in heads:
        qh = jnp.where(in_head[h], q2, jnp.zeros_like(q2))
        gate_t = lax.dot_general(km_ref[:nb8, :], qh.astype(F32), _NT, precision=lax.Precision.HIGHEST,
                                 preferred_element_type=F32)
        bias_t = _top3_bias(gate_t, cand_f, own_f, axis=0)
        bias_t = jnp.where(cand == own, 0.0, bias_t)
        bias = jnp.concatenate([bias_t, jnp.zeros((LANES - nb8, tq), F32)], axis=0).T
        q_aug.append(jnp.concatenate([qh, bias.astype(BF16)], axis=1))

    def logits(h, g):
        return jnp.dot(q_aug[h], kb_ref[g], preferred_element_type=F32)

    def stage_a(g, s_buf, mg_buf, masked):
        for h in heads:
            s = logits(h, g)
            if masked:
                s = jnp.where(causal, s, NEG)
            s_buf[h] = s
            mg_buf[h] = _tree(jnp.maximum, _lane_tiles(s))

    def stage_b(s_buf, mg_buf, g):
        vg = vb_ref[g]
        for h in heads:
            m_old = m_ref[h]
            m_new = jnp.maximum(m_old, jnp.broadcast_to(jnp.max(mg_buf[h], axis=1, keepdims=True),
                                                        (tq, LANES)))
            alpha = jnp.exp2(m_old - m_new)
            p = jnp.exp2(s_buf[h] - jnp.concatenate([m_new] * (gk // LANES), axis=1))
            m_ref[h] = m_new
            l_ref[h] = alpha * l_ref[h] + _tree(jnp.add, _lane_tiles(p))
            acc_ref[h] = alpha * acc_ref[h] + lax.dot_general(p.astype(BF16), vg, _NT,
                                                              preferred_element_type=F32)

    m_ref[...] = jnp.full(m_ref.shape, NEG, F32)
    l_ref[...] = jnp.zeros(l_ref.shape, F32)
    acc_ref[...] = jnp.zeros(acc_ref.shape, F32)

    stage_a(g_last, s0_ref, mg0_ref, True)
    n_pairs = g_last // 2

    def pair_body(j, _):
        stage_b(s0_ref, mg0_ref, jnp.where(j == 0, g_last, 2 * j - 1))
        stage_a(2 * j, s1_ref, mg1_ref, False)
        stage_b(s1_ref, mg1_ref, 2 * j)
        stage_a(2 * j + 1, s0_ref, mg0_ref, False)
        return 0

    lax.fori_loop(0, n_pairs, pair_body, 0)
    g_buffered = jnp.where(n_pairs == 0, g_last, 2 * n_pairs - 1)

    @pl.when(g_last % 2 == 1)
    def _():
        stage_b(s0_ref, mg0_ref, g_buffered)
        stage_a(g_last - 1, s1_ref, mg1_ref, False)
        stage_b(s1_ref, mg1_ref, g_last - 1)

    @pl.when(g_last % 2 == 0)
    def _():
        stage_b(s0_ref, mg0_ref, g_buffered)

    out = None
    for h in heads:
        oh = acc_ref[h] / jnp.sum(l_ref[h], axis=1, keepdims=True)
        out = oh if out is None else jnp.where(in_head[h], oh, out)
    o_ref[...] = out.astype(BF16)


GROUP_BLOCKS = 4
Q_TILE_BLOCKS = 2


def _moba_prompt(q_bf, k_t, v_t, *, bsz, seq, layer):
    nblk = seq // MOBA_BLOCK
    n = bsz * seq
    gb = min(GROUP_BLOCKS, nblk)
    gk = gb * MOBA_BLOCK
    tq = min(Q_TILE_BLOCKS, gb) * MOBA_BLOCK
    n_tiles = seq // tq
    assert nblk <= LANES and nblk % gb == 0 and gk % tq == 0
    hp = HEADS_PER_GROUP
    return pl.pallas_call(
        _moba_prompt_kernel,
        grid=(bsz, N_GROUPS, n_tiles),
        in_specs=[pl.BlockSpec((tq, LANES), lambda b, g, i: (b * n_tiles + i, g)),
                  pl.BlockSpec((1, 1, LANES, seq), lambda b, g, i: (layer, b, g, 0)),
                  pl.BlockSpec((1, 1, LANES, seq), lambda b, g, i: (layer, b, g, 0))],
        out_specs=pl.BlockSpec((tq, LANES), lambda b, g, i: (b * n_tiles + i, g)),
        out_shape=jax.ShapeDtypeStruct((n, D_ATTN), BF16),
        scratch_shapes=[pltpu.VMEM((nblk // gb, 2 * LANES, gk), BF16),
                        pltpu.VMEM((nblk // gb, LANES, gk), BF16),
                        pltpu.VMEM((LANES, LANES), F32)]
        + [pltpu.VMEM((hp, tq, gk), F32)] * 2
        + [pltpu.VMEM((hp, tq, LANES), F32)] * 5,
        compiler_params=_params(("arbitrary", "arbitrary", "arbitrary")),
        name="moba_prompt",
    )(q_bf, k_t, v_t)


def _moba_sample_kernel(pt_ref, q_ref, kn_ref, vn_ref, ck_hbm, cv_hbm, o_ref,
                        buf, sem, s_ref, p_ref, *, layer, n_pages, page, pages_per_chunk):
    b = pl.program_id(0)
    n_q = q_ref.shape[1]
    rows = n_q * N_HEADS
    chunk = pages_per_chunk * page
    n_chunks = n_pages // pages_per_chunk
    blocks_per_chunk = chunk // MOBA_BLOCK
    nblk = n_pages * page // MOBA_BLOCK

    n_slots = buf.shape[0]
    ahead = n_slots - 1
    n_loads = 2 * n_chunks
    assert n_loads % n_slots == 0 and ahead <= n_loads
    loads = [(ck_hbm, c) for c in range(n_chunks)] + [(cv_hbm, c) for c in range(n_chunks)]

    def chunk_copies(row, i):
        src, c = loads[i]
        return [pltpu.make_async_copy(src.at[layer, pt_ref[row, c * pages_per_chunk + j]],
                                      buf.at[i % n_slots, j], sem.at[i % n_slots])
                for j in range(pages_per_chunk)]

    def start(row, i):
        for cp in chunk_copies(row, i):
            cp.start()

    def start_ahead(i):
        nxt = i + ahead
        if nxt < n_loads:
            start(b, nxt)
        else:
            @pl.when(b + 1 < pl.num_programs(0))
            def _():
                start(b + 1, nxt - n_loads)

    def wait(i):
        for cp in chunk_copies(b, i):
            cp.wait()

    def chunk_value(i):
        return jnp.concatenate([buf[i % n_slots, j] for j in range(pages_per_chunk)], axis=1).astype(BF16)

    @pl.when(b == 0)
    def _():
        for i in range(ahead):
            start(b, i)

    q4 = q_ref[0]
    row_i = lax.broadcasted_iota(jnp.int32, (rows, D_ATTN), 0)
    col_i = lax.broadcasted_iota(jnp.int32, (rows, D_ATTN), 1)
    own_head = (col_i // HEAD_DIM) == (row_i % N_HEADS)
    q_rep = jnp.concatenate([jnp.broadcast_to(q4[t:t + 1], (N_HEADS, D_ATTN)) for t in range(n_q)], axis=0)
    q_bd = jnp.where(own_head, q_rep, jnp.zeros_like(q_rep))

    lane = lax.broadcasted_iota(jnp.int32, (rows, LANES), 1)
    gate = jnp.zeros((rows, LANES), F32)
    for i in range(n_chunks):
        wait(i)
        start_ahead(i)
        kc = chunk_value(i)
        s_c = jnp.dot(q_bd, kc, preferred_element_type=F32)
        s_ref[:, i * chunk:(i + 1) * chunk] = s_c
        for j in range(blocks_per_chunk):
            tot = jnp.sum(s_c[:, j * MOBA_BLOCK:(j + 1) * MOBA_BLOCK], axis=1, keepdims=True)
            gate = jnp.where(lane == i * blocks_per_chunk + j, tot * (1.0 / MOBA_BLOCK), gate)

    bias = _top3_bias(gate, lane.astype(F32), float(nblk), axis=1)

    tok = lax.broadcasted_iota(jnp.int32, (rows, 1), 0) // N_HEADS
    q_bd_f = q_bd.astype(F32)
    kn = kn_ref[0]
    vn = vn_ref[0]
    s_new = []
    for j in range(n_q):
        sj = jnp.sum(q_bd_f * kn[j:j + 1, :], axis=1, keepdims=True)
        s_new.append(jnp.where(tok >= j, sj, NEG))

    m = s_new[0]
    for sj in s_new[1:]:
        m = jnp.maximum(m, sj)
    bias_cols = []
    for n in range(nblk):
        bcol = jnp.max(jnp.where(lane == n, bias, NEG), axis=1, keepdims=True)
        bias_cols.append(bcol)
        sb = s_ref[:, n * MOBA_BLOCK:(n + 1) * MOBA_BLOCK] + bcol
        m = jnp.maximum(m, jnp.max(sb, axis=1, keepdims=True))
    p_new = [jnp.exp(sj - m) for sj in s_new]
    l = p_new[0]
    for pj in p_new[1:]:
        l = l + pj
    for n in range(nblk):
        pb = jnp.exp(s_ref[:, n * MOBA_BLOCK:(n + 1) * MOBA_BLOCK] + bias_cols[n] - m)
        l = l + jnp.sum(pb, axis=1, keepdims=True)
        p_ref[:, n * MOBA_BLOCK:(n + 1) * MOBA_BLOCK] = pb.astype(BF16)

    acc = jnp.zeros((rows, D_ATTN), F32)
    for i in range(n_chunks, 2 * n_chunks):
        wait(i)
        start_ahead(i)
        c = i - n_chunks
        vc = chunk_value(i)
        acc = acc + lax.dot_general(p_ref[:, c * chunk:(c + 1) * chunk], vc, _NT,
                                    preferred_element_type=F32)
    for j in range(n_q):
        acc = acc + p_new[j] * vn[j:j + 1, :]
    out = jnp.where(own_head, acc / l, 0.0)
    out = jnp.sum(out.reshape(n_q, N_HEADS, D_ATTN), axis=1)
    o_ref[0] = out.astype(BF16)


SAMPLE_SLOTS = 4


def _moba_sample(page_table, q_bf, k_new, v_new, ck_t, cv_t, *, layer):
    bsz, n_pages = page_table.shape
    page = ck_t.shape[3]
    n_q = q_bf.shape[0] // bsz
    rows = n_q * N_HEADS
    pages_per_chunk = min(8, n_pages)
    chunk = pages_per_chunk * page
    n_slots = min(SAMPLE_SLOTS, 2 * n_pages // pages_per_chunk)
    past = n_pages * page
    assert n_pages % pages_per_chunk == 0 and chunk % MOBA_BLOCK == 0 and past // MOBA_BLOCK <= LANES
    blk3 = lambda b, pt: (b, 0, 0)
    kernel = functools.partial(_moba_sample_kernel, layer=layer, n_pages=n_pages, page=page,
                               pages_per_chunk=pages_per_chunk)
    out = pl.pallas_call(
        kernel,
        grid_spec=pltpu.PrefetchScalarGridSpec(
            num_scalar_prefetch=1,
            grid=(bsz,),
            in_specs=[pl.BlockSpec((1, n_q, D_ATTN), blk3),
                      pl.BlockSpec((1, n_q, D_ATTN), blk3),
                      pl.BlockSpec((1, n_q, D_ATTN), blk3),
                      pl.BlockSpec(memory_space=pl.ANY),
                      pl.BlockSpec(memory_space=pl.ANY)],
            out_specs=pl.BlockSpec((1, n_q, D_ATTN), blk3),
            scratch_shapes=[pltpu.VMEM((n_slots, pages_per_chunk, D_ATTN, page), F32),
                            pltpu.SemaphoreType.DMA((n_slots,)),
                            pltpu.VMEM((rows, past), F32),
                            pltpu.VMEM((rows, past), BF16)]),
        out_shape=jax.ShapeDtypeStruct((bsz, n_q, D_ATTN), BF16),
        compiler_params=_params(("arbitrary",)),
        name="moba_sample",
    )(page_table, q_bf.reshape(bsz, n_q, D_ATTN), k_new.reshape(bsz, n_q, D_ATTN),
      v_new.reshape(bsz, n_q, D_ATTN), ck_t, cv_t)
    return out.reshape(bsz * n_q, D_ATTN)


def _rec_sample_kernel(xr_ref, gr_ref, prev_ref, h0_ref, cw_ref, cb_ref, wa_ref, ba_ref, wx_ref, bx_ref,
                       lam_ref, rec_ref, h_ref, *, steps):
    x = xr_ref[...]
    prev = prev_ref[...]
    n = x.shape[0]
    t = lax.broadcasted_iota(jnp.int32, x.shape, 0) % steps
    cw = cw_ref[...]
    xc = cb_ref[...] + x * cw[CONV_W - 1:CONV_W, :]
    for d in range(1, CONV_W):
        from_x = pltpu.roll(x, d, axis=0)
        from_prev = pltpu.roll(prev, (n - (CONV_W - 1 - d)) % n, axis=0)
        xc = xc + jnp.where(t >= d, from_x, from_prev) * cw[CONV_W - 1 - d:CONV_W - d, :]

    a, bt = _lru_terms(xc, wa_ref, ba_ref, wx_ref, bx_ref, lam_ref)
    d = 1
    while d < steps:
        keep = t >= d
        a_sh = pltpu.roll(a, d, axis=0)
        b_sh = pltpu.roll(bt, d, axis=0)
        bt = jnp.where(keep, a * b_sh + bt, bt)
        a = jnp.where(keep, a * a_sh, a)
        d *= 2
    h = a * h0_ref[...] + bt
    h_ref[...] = h
    rec_ref[...] = (h * jax.nn.gelu(gr_ref[...])).astype(BF16)


def _rec_sample(xr, gr, prev, h0_rep, cw, cb, wa_bd, ba, wx_bd, bx, lam, *, steps):
    n = xr.shape[0]
    full = lambda shape: pl.BlockSpec(shape, lambda i: (0,) * len(shape))
    return pl.pallas_call(
        functools.partial(_rec_sample_kernel, steps=steps),
        grid=(1,),
        in_specs=[full((n, D_REC))] * 4 + [full((CONV_W, D_REC)), full((1, D_REC)),
                                            full((D_REC, D_REC)), full((1, D_REC)),
                                            full((D_REC, D_REC)), full((1, D_REC)), full((1, D_REC))],
        out_specs=[full((n, D_REC)), full((n, D_REC))],
        out_shape=[jax.ShapeDtypeStruct((n, D_REC), BF16), jax.ShapeDtypeStruct((n, D_REC), F32)],
        compiler_params=_params(("arbitrary",)),
        name="rec_sample",
    )(xr, gr, prev, h0_rep, cw, cb, wa_bd, ba, wx_bd, bx, lam)


MXU_DIM = 256
_FF_SPLIT = (D_FF // MXU_DIM + 1) // 2 * MXU_DIM
FF_CHUNKS = ((0, _FF_SPLIT), (_FF_SPLIT, D_FF))


def _out_ffn_kernel(x_ref, attn_ref, rec_ref, wo_ref, gf_ref, wg_ref, wu_ref, wd_ref, gl_ref, o_ref, *, final):
    mixed = jnp.concatenate([attn_ref[...], rec_ref[...]], axis=1)
    x1 = x_ref[...] + jnp.dot(mixed, wo_ref[...], preferred_element_type=F32)
    hn = _rmsnorm(x1, gf_ref[...]).astype(BF16)
    y = x1
    for lo, hi in FF_CHUNKS:
        g = jnp.dot(hn, wg_ref[:, lo:hi], preferred_element_type=F32)
        u = jnp.dot(hn, wu_ref[:, lo:hi], preferred_element_type=F32)
        act = (jax.nn.silu(g) * u).astype(BF16)
        y = y + jnp.dot(act, wd_ref[lo:hi, :], preferred_element_type=F32)
    if final:
        y = _rmsnorm(y, gl_ref[...])
    o_ref[...] = y


def _out_ffn(x, attn, rec, wo, gf, wg, wu, wd, gl, *, tm, final):
    n = x.shape[0]
    row = lambda i: (i, 0)
    return pl.pallas_call(
        functools.partial(_out_ffn_kernel, final=final),
        grid=(n // tm,),
        in_specs=[pl.BlockSpec((tm, D_MODEL), row), pl.BlockSpec((tm, D_ATTN), row),
                  pl.BlockSpec((tm, D_REC), row),
                  _resident((D_ATTN + D_REC, D_MODEL)), _resident((1, D_MODEL)),
                  _resident((D_MODEL, D_FF)), _resident((D_MODEL, D_FF)), _resident((D_FF, D_MODEL)),
                  _resident((1, D_MODEL))],
        out_specs=pl.BlockSpec((tm, D_MODEL), row),
        out_shape=jax.ShapeDtypeStruct((n, D_MODEL), F32),
        compiler_params=_params(("parallel",)),
        name="out_ffn",
    )(x, attn, rec, wo, gf, wg, wu, wd, gl)


def _block_diag(w):
    h, n, _ = w.shape
    eye = jnp.eye(h, dtype=w.dtype)
    return (eye[:, None, :, None] * w[:, :, None, :]).reshape(h * n, h * n)


def kernel(x_prompt, x_sample, cache_k, cache_v, state_h, state_conv, page_table, norm_mix, w_in, conv_w, conv_b, w_gate_a, b_gate_a, w_gate_x, b_gate_x, lru_lambda, w_out, norm_ffn, w_ff_gate, w_ff_up, w_ff_down, norm_final):
    bp, sp, _ = x_prompt.shape
    bs, ss, _ = x_sample.shape
    depth = w_in.shape[0]
    n_phys, page = cache_k.shape[1:3]
    past = page_table.shape[1] * page
    assert sp % MOBA_BLOCK == 0 and past % MOBA_BLOCK == 0 and ss >= CONV_W - 1

    tm_p = 512
    pos_p = jnp.arange(sp, dtype=jnp.int32)
    tab_p, tab_pt = _rope_table(pos_p), _rope_table_t(pos_p)
    tab_s = _rope_table(past + (jnp.arange(bs * ss, dtype=jnp.int32) % ss))

    ck_t = jnp.transpose(cache_k, (0, 1, 3, 4, 2)).reshape(depth, n_phys, D_ATTN, page)
    cv_t = jnp.transpose(cache_v, (0, 1, 3, 4, 2)).reshape(depth, n_phys, D_ATTN, page)

    row2 = lambda a: a.reshape(1, -1)
    xp = x_prompt.reshape(bp * sp, D_MODEL)
    xs = x_sample.reshape(bs * ss, D_MODEL)
    gl = row2(norm_final)
    hp_l, cp_l, ks_l, vs_l, hs_l, cs_l = [], [], [], [], [], []
    kv_t = (jnp.zeros((depth, bp, D_ATTN, sp), F32), jnp.zeros((depth, bp, D_ATTN, sp), F32))
    for l in range(depth):
        final = l == depth - 1
        w_in_bf = w_in[l].astype(BF16)
        wq = w_in_bf[:, :D_ATTN]
        wkv_t = w_in_bf[:, D_ATTN:3 * D_ATTN].T
        wr = w_in_bf[:, 3 * D_ATTN:]
        wo_bf = w_out[l].astype(BF16)
        wg_bf, wu_bf, wd_bf = w_ff_gate[l].astype(BF16), w_ff_up[l].astype(BF16), w_ff_down[l].astype(BF16)
        wa_bd = _block_diag(w_gate_a[l]).astype(BF16)
        wx_bd = _block_diag(w_gate_x[l]).astype(BF16)
        g_mix, g_ffn = row2(norm_mix[l]), row2(norm_ffn[l])
        rec_w = (conv_w[l], row2(conv_b[l]), wa_bd, row2(b_gate_a[l]), wx_bd, row2(b_gate_x[l]),
                 row2(lru_lambda[l]))

        q_bf, k_t, v_t, rec, h_last, conv_tail = _proj_rec_prompt(
            xp, g_mix, wq, wkv_t, wr, tab_p, tab_pt, rec_w, kv_t, bsz=bp, seq=sp, tm=tm_p, layer=l, depth=depth)
        kv_t = (k_t, v_t)
        attn = _moba_prompt(q_bf, k_t, v_t, bsz=bp, seq=sp, layer=l)
        xp = _out_ffn(xp, attn, rec, wo_bf, g_ffn, wg_bf, wu_bf, wd_bf, gl, tm=tm_p, final=final)
        hp_l.append(h_last.reshape(bp, D_REC))
        cp_l.append(conv_tail[:, SUBLANES - (CONV_W - 1):])

        q_bf, k, v, xr, gr = _proj_sample(xs, g_mix, w_in_bf, tab_s)
        attn = _moba_sample(page_table, q_bf, k, v, ck_t, cv_t, layer=l)
        prev = jnp.pad(state_conv[l], ((0, 0), (0, ss - (CONV_W - 1)), (0, 0))).reshape(bs * ss, D_REC)
        h0_rep = jnp.repeat(state_h[l], ss, axis=0)
        rec, h_all = _rec_sample(xr, gr, prev, h0_rep, *rec_w, steps=ss)
        xs = _out_ffn(xs, attn, rec, wo_bf, g_ffn, wg_bf, wu_bf, wd_bf, gl, tm=bs * ss, final=final)
        ks_l.append(k.reshape(bs, ss, N_HEADS, HEAD_DIM))
        vs_l.append(v.reshape(bs, ss, N_HEADS, HEAD_DIM))
        hs_l.append(h_all.reshape(bs, ss, D_REC)[:, ss - 1])
        cs_l.append(xr.reshape(bs, ss, D_REC)[:, ss - (CONV_W - 1):])

    def token_major(t):
        return jnp.transpose(t.reshape(depth, bp, N_HEADS, HEAD_DIM, sp), (0, 1, 4, 2, 3))

    return (xp.reshape(bp, sp, D_MODEL), xs.reshape(bs, ss, D_MODEL),
            token_major(kv_t[0]), token_major(kv_t[1]), jnp.stack(hp_l), jnp.stack(cp_l),
            jnp.stack(ks_l), jnp.stack(vs_l), jnp.stack(hs_l), jnp.stack(cs_l))
```
